```python
import jax, jax.numpy as jnp
from jax import lax
import numpy as np

D_MODEL = 1024
BATCH = 8
SEQ = 8192
DEPTH = 2

GRID_W = 64
CTX_LEN = 256
HEAD_DIM = 64
ATTN_WIDTH = D_MODEL // 2
N_HEADS = ATTN_WIDTH // HEAD_DIM
N_KV_HEADS = N_HEADS // 4
GQA_GROUP = N_HEADS // N_KV_HEADS
KV_WIDTH = N_KV_HEADS * HEAD_DIM
ROPE_PAIRS = HEAD_DIM // 4
ROPE_THETA = 10000.0
Q_BLOCK = 128
ATTN_SCALE = HEAD_DIM ** -0.5
LRU_WIDTH = D_MODEL // 4
LRU_BLOCKS = 4
LRU_BLOCK = LRU_WIDTH // LRU_BLOCKS
LRU_CONV_W = 4
LRU_PAD = (2, 1)
LRU_C = 8.0
SC_WIDTH = D_MODEL // 4
SC_CONV_W = 3
SC_PAD = (1, 1)
MIX_WIDTH = ATTN_WIDTH + LRU_WIDTH + SC_WIDTH
IN_WIDTH = ATTN_WIDTH + 2 * KV_WIDTH + 2 * LRU_WIDTH + 3 * SC_WIDTH
D_FF = (8 * D_MODEL // 3 + 127) // 128 * 128
N_MOD = 9
EPS = 1e-6

kernel_name = 'hymba_style_diffusion_hybrid_block'


def _rms(x):
    xf = x.astype(jnp.float32)
    return (xf * lax.rsqrt(jnp.mean(xf * xf, axis=-1, keepdims=True) + EPS)).astype(x.dtype)


def rms_norm(x, g):
    return _rms(x) * g


def ada_norm(x, g, shift, scale):
    return rms_norm(x, g) * (1 + scale[:, None, :]) + shift[:, None, :]


def swiglu(h, w_in, w_out):
    gate, up = jnp.split(h @ w_in, 2, axis=-1)
    return (jax.nn.silu(gate) * up) @ w_out


def dw_conv(x, w, b, pad):
    y = lax.conv_general_dilated(x, w[:, None, :], (1,), [pad],
                                 dimension_numbers=('NWC', 'WIO', 'NWC'),
                                 feature_group_count=x.shape[-1])
    return y + b


def axial_rope_tables(seq):
    rows = seq // GRID_W
    row_ids = jnp.repeat(jnp.arange(rows), GRID_W).astype(jnp.float32)
    col_ids = jnp.tile(jnp.arange(GRID_W), rows).astype(jnp.float32)
    inv_freq = ROPE_THETA ** (-jnp.arange(ROPE_PAIRS, dtype=jnp.float32) / ROPE_PAIRS)
    ang = jnp.stack([row_ids[:, None] * inv_freq, col_ids[:, None] * inv_freq], axis=1)
    return jnp.cos(ang), jnp.sin(ang)


def apply_rope(x, cos, sin):
    b, s, h, _ = x.shape
    xr = x.astype(jnp.float32).reshape(b, s, h, 2, 2, ROPE_PAIRS)
    x1, x2 = xr[..., 0, :], xr[..., 1, :]
    cs, sn = cos[None, :, None], sin[None, :, None]
    out = jnp.stack([x1 * cs - x2 * sn, x2 * cs + x1 * sn], axis=-2)
    return out.reshape(b, s, h, HEAD_DIM).astype(x.dtype)


def head_rms(x, g):
    return rms_norm(x.reshape(*x.shape[:-1], -1, HEAD_DIM), g)


def attend(q, k, v):
    s = jnp.einsum('bqkgd,bskd->bkgqs', q, k, preferred_element_type=jnp.float32) * ATTN_SCALE
    p = jax.nn.softmax(s, axis=-1).astype(v.dtype)
    return jnp.einsum('bkgqs,bskd->bqkgd', p, v)


def blocked_attention(q, k, v):
    b, s = q.shape[:2]
    nb = s // Q_BLOCK
    qb = q.reshape(b, nb, Q_BLOCK, N_KV_HEADS, GQA_GROUP, HEAD_DIM).swapaxes(0, 1)
    o = lax.map(lambda qblk: attend(qblk, k, v), qb)
    return o.swapaxes(0, 1).reshape(b, s, ATTN_WIDTH)


def _lin_combine(left, right):
    a1, b1 = left
    a2, b2 = right
    return a1 * a2, a2 * b1 + b2


def lru_scan(u, wa, ba, wx, bx, lam, h0, reverse):
    b, l, w = u.shape
    ub = u.reshape(b, l, LRU_BLOCKS, LRU_BLOCK)
    r = jax.nn.sigmoid((jnp.einsum('blnd,nde->blne', ub, wa).reshape(b, l, w) + ba).astype(jnp.float32))
    i = jax.nn.sigmoid((jnp.einsum('blnd,nde->blne', ub, wx).reshape(b, l, w) + bx).astype(jnp.float32))
    log_a = -LRU_C * r * jax.nn.softplus(-lam.astype(jnp.float32))
    a = jnp.exp(log_a)
    xin = jnp.sqrt(-jnp.expm1(2 * log_a)) * i * u.astype(jnp.float32)
    edge = l - 1 if reverse else 0
    xin = xin.at[:, edge].add(a[:, edge] * h0)
    _, h = lax.associative_scan(_lin_combine, (a, xin), reverse=reverse, axis=1)
    return h


def lru_dir(u, p, d, h0, reverse):
    return lru_scan(u, p['wa'][d], p['ba'][d], p['wx'][d], p['bx'][d], p['lam'][d], h0, reverse)


def merge_groups(parts, g, w_out):
    return jnp.concatenate([_rms(t) for t in parts], axis=-1) * g @ w_out


def mixer(hx, hc, p, cos, sin, with_ctx):
    bsz, seq, _ = hx.shape
    n_ctx = hc.shape[1]
    cuts = [ATTN_WIDTH, ATTN_WIDTH + KV_WIDTH, ATTN_WIDTH + 2 * KV_WIDTH]
    cuts += [cuts[-1] + LRU_WIDTH, cuts[-1] + 2 * LRU_WIDTH]
    cuts += [cuts[-1] + SC_WIDTH, cuts[-1] + 2 * SC_WIDTH]
    qx, kx, vx, ux, gx, bgx, cgx, sx = jnp.split(hx @ p['w_in'], cuts, axis=-1)
    qc, kc, vc, uc, gc, bgc, cgc, sc = jnp.split(hc @ p['w_in'], cuts, axis=-1)

    qx = apply_rope(head_rms(qx, p['q_g']), cos, sin).reshape(bsz, seq, N_KV_HEADS, GQA_GROUP, HEAD_DIM)
    kx = apply_rope(head_rms(kx, p['k_g']), cos, sin)
    kc = head_rms(kc, p['k_g'])
    vx = vx.reshape(bsz, seq, N_KV_HEADS, HEAD_DIM)
    vc = vc.reshape(bsz, n_ctx, N_KV_HEADS, HEAD_DIM)
    k_all = jnp.concatenate([kc, kx], axis=1)
    v_all = jnp.concatenate([vc, vx], axis=1)
    attn_x = blocked_attention(qx, k_all, v_all)

    ux = dw_conv(ux, p['lru_conv_w'], p['lru_conv_b'], LRU_PAD)
    uc = dw_conv(uc, p['lru_conv_w'], p['lru_conv_b'], LRU_PAD)
    h0 = jnp.zeros((bsz, LRU_WIDTH), jnp.float32)
    hc_f = lru_dir(uc, p, 0, h0, False)
    hc_b = lru_dir(uc, p, 1, h0, True)
    hx_f = lru_dir(ux, p, 0, hc_f[:, -1], False)
    hx_b = lru_dir(ux, p, 1, hc_b[:, 0], True)
    lru_x = (hx_f + hx_b).astype(hx.dtype) * jax.nn.gelu(gx)

    sc_x = bgx * dw_conv(cgx * sx, p['sc_conv_w'], p['sc_conv_b'], SC_PAD)

    out_x = merge_groups([attn_x, lru_x, sc_x], p['grp_g'], p['w_out'])
    if not with_ctx:
        return out_x, None

    qc = head_rms(qc, p['q_g']).reshape(bsz, n_ctx, N_KV_HEADS, GQA_GROUP, HEAD_DIM)
    attn_c = attend(qc, kc, vc).reshape(bsz, n_ctx, ATTN_WIDTH)
    lru_c = (hc_f + hc_b).astype(hc.dtype) * jax.nn.gelu(gc)
    sc_c = bgc * dw_conv(cgc * sc, p['sc_conv_w'], p['sc_conv_b'], SC_PAD)
    out_c = merge_groups([attn_c, lru_c, sc_c], p['grp_g'], p['w_out'])
    return out_x, out_c


def setup_inputs(seed: int = 0) -> dict:
    key = jax.random.key(seed)
    ks = jax.random.split(key, 24)
    f32 = jnp.float32

    def nrm(k, shape, scale):
        return jax.random.normal(k, shape, f32) * scale

    a_pow = jax.random.uniform(ks[18], (DEPTH, 2, LRU_WIDTH), f32, 0.9, 0.999)
    a_base = a_pow ** (1.0 / LRU_C)
    return {
        'x': nrm(ks[0], (BATCH, SEQ, D_MODEL), 1.0),
        'c': nrm(ks[1], (BATCH, D_MODEL), 1.0),
        'ctx': nrm(ks[2], (BATCH, CTX_LEN, D_MODEL), 1.0),
        'c_ctx': nrm(ks[3], (D_MODEL,), 1.0),
        'w_mod': nrm(ks[4], (DEPTH, D_MODEL, N_MOD * D_MODEL), 0.5 * D_MODEL ** -0.5),
        'b_mod': nrm(ks[5], (DEPTH, N_MOD * D_MODEL), 0.02),
        'norm_g': 1.0 + nrm(ks[6], (DEPTH, 3, D_MODEL), 0.02),
        'w_ffn_in': nrm(ks[7], (DEPTH, 2, D_MODEL, 2 * D_FF), D_MODEL ** -0.5),
        'w_ffn_out': nrm(ks[8], (DEPTH, 2, D_FF, D_MODEL), D_FF ** -0.5),
        'w_in': nrm(ks[9], (DEPTH, D_MODEL, IN_WIDTH), D_MODEL ** -0.5),
        'q_norm_g': 1.0 + nrm(ks[10], (DEPTH, HEAD_DIM), 0.02),
        'k_norm_g': 1.0 + nrm(ks[11], (DEPTH, HEAD_DIM), 0.02),
        'lru_conv_w': nrm(ks[12], (DEPTH, LRU_CONV_W, LRU_WIDTH), LRU_CONV_W ** -0.5),
        'lru_conv_b': nrm(ks[13], (DEPTH, LRU_WIDTH), 0.02),
        'lru_wa': nrm(ks[14], (DEPTH, 2, LRU_BLOCKS, LRU_BLOCK, LRU_BLOCK), LRU_BLOCK ** -0.5),
        'lru_ba': nrm(ks[15], (DEPTH, 2, LRU_WIDTH), 0.02),
        'lru_wx': nrm(ks[16], (DEPTH, 2, LRU_BLOCKS, LRU_BLOCK, LRU_BLOCK), LRU_BLOCK ** -0.5),
        'lru_bx': nrm(ks[17], (DEPTH, 2, LRU_WIDTH), 0.02),
        'lru_lambda': jnp.log(a_base) - jnp.log1p(-a_base),
        'sc_conv_w': nrm(ks[19], (DEPTH, SC_CONV_W, SC_WIDTH), SC_CONV_W ** -0.5),
        'sc_conv_b': nrm(ks[20], (DEPTH, SC_WIDTH), 0.02),
        'grp_norm_g': 1.0 + nrm(ks[21], (DEPTH, MIX_WIDTH), 0.02),
        'w_out': nrm(ks[22], (DEPTH, MIX_WIDTH, D_MODEL), MIX_WIDTH ** -0.5),
        'final_norm_g': 1.0 + nrm(ks[23], (D_MODEL,), 0.02),
    }


def reference(x, c, ctx, c_ctx, w_mod, b_mod, norm_g, w_ffn_in, w_ffn_out, w_in,
              q_norm_g, k_norm_g, lru_conv_w, lru_conv_b, lru_wa, lru_ba, lru_wx, lru_bx,
              lru_lambda, sc_conv_w, sc_conv_b, grp_norm_g, w_out, final_norm_g):
    bsz = x.shape[0]
    cos, sin = axial_rope_tables(x.shape[1])
    h_ctx = ctx
    for l in range(DEPTH):
        last = l == DEPTH - 1
        mx = (jax.nn.silu(c) @ w_mod[l] + b_mod[l]).reshape(bsz, N_MOD, D_MODEL)
        mc = (jax.nn.silu(c_ctx) @ w_mod[l] + b_mod[l]).reshape(1, N_MOD, D_MODEL)

        x = x + 0.5 * mx[:, 2, None] * swiglu(ada_norm(x, norm_g[l, 0], mx[:, 0], mx[:, 1]),
                                               w_ffn_in[l, 0], w_ffn_out[l, 0])
        h_ctx = h_ctx + 0.5 * mc[:, 2, None] * swiglu(ada_norm(h_ctx, norm_g[l, 0], mc[:, 0], mc[:, 1]),
                                                       w_ffn_in[l, 0], w_ffn_out[l, 0])

        p = {'w_in': w_in[l], 'q_g': q_norm_g[l], 'k_g': k_norm_g[l],
             'lru_conv_w': lru_conv_w[l], 'lru_conv_b': lru_conv_b[l],
             'wa': lru_wa[l], 'ba': lru_ba[l], 'wx': lru_wx[l], 'bx': lru_bx[l], 'lam': lru_lambda[l],
             'sc_conv_w': sc_conv_w[l], 'sc_conv_b': sc_conv_b[l],
             'grp_g': grp_norm_g[l], 'w_out': w_out[l]}
        hx = ada_norm(x, norm_g[l, 1], mx[:, 3], mx[:, 4])
        hc = ada_norm(h_ctx, norm_g[l, 1], mc[:, 3], mc[:, 4])
        out_x, out_c = mixer(hx, hc, p, cos, sin, not last)
        x = x + mx[:, 5, None] * out_x

        x = x + 0.5 * mx[:, 8, None] * swiglu(ada_norm(x, norm_g[l, 2], mx[:, 6], mx[:, 7]),
                                               w_ffn_in[l, 1], w_ffn_out[l, 1])
        if not last:
            h_ctx = h_ctx + mc[:, 5, None] * out_c
            h_ctx = h_ctx + 0.5 * mc[:, 8, None] * swiglu(ada_norm(h_ctx, norm_g[l, 2], mc[:, 6], mc[:, 7]),
                                                           w_ffn_in[l, 1], w_ffn_out[l, 1])
    return rms_norm(x, final_norm_g)
```

```python
import functools
import math

import jax
import jax.numpy as jnp
from jax import lax
from jax.experimental import pallas as pl
from jax.experimental.pallas import tpu as pltpu

F32 = jnp.float32
BF16 = jnp.bfloat16

HEAD_DIM = 64
GQA_GROUP = 4
ROPE_PAIRS = HEAD_DIM // 4
GRID_W = 64
ROPE_THETA = 10000.0
N_MOD = 9
EPS = 1e-6
LRU_C = 8.0
LRU_CONV_W = 4
LRU_PAD_LEFT = 2
SC_CONV_W = 3
SC_PAD_LEFT = 1
Q_SCALE = HEAD_DIM ** -0.5 * math.log2(math.e)

V7X_SUBLANES = 8
V7X_LANES = 128
V7X_BF16_ROWS = 16
V7X_VMEM_BYTES = 64 * 2 ** 20
VMEM_LIMIT_BYTES = V7X_VMEM_BYTES * 7 // 8

FFN_ROWS = 256
PROJ_ROWS = 512
ATTN_Q_ROWS = 512
ATTN_K_ROWS = 768
LRU_ROWS = 512
MERGE_ROWS = 512
HALO = V7X_SUBLANES


def _tile(length, target, align):
    best = None
    for t in range(align, min(length, target) + 1, align):
        if length % t == 0:
            best = t
    if best is None:
        raise ValueError(f"no {align}-aligned tile divides {length}")
    return best


def _params(*semantics):
    return pltpu.CompilerParams(dimension_semantics=semantics, vmem_limit_bytes=VMEM_LIMIT_BYTES)


def _resident(shape):
    zeros = (0,) * len(shape)
    return pl.BlockSpec(shape, lambda *_: zeros, pipeline_mode=pl.Buffered(1))


def _rms(t):
    return t * lax.rsqrt(jnp.mean(t * t, axis=-1, keepdims=True) + EPS)


def _ada_norm(x, g, shift, scale):
    return (_rms(x) * g) * (1.0 + scale) + shift


def _gelu_tanh(x):
    cdf = 0.5 * (1.0 + jnp.tanh(math.sqrt(2.0 / math.pi) * (x + 0.044715 * (x * x * x))))
    return x * cdf


def _mod_kernel(c_ref, w_ref, b_ref, o_ref):
    c = c_ref[...]
    a = (c * jax.nn.sigmoid(c)).astype(BF16)
    o_ref[0] = jnp.dot(a, w_ref[0].astype(BF16), preferred_element_type=F32) + b_ref[0]


def _modulation(c_all, w_mod, b_mod):
    depth, d, n = w_mod.shape
    rows = c_all.shape[0]
    tn = _tile(n, d, V7X_LANES)
    out = pl.pallas_call(
        _mod_kernel,
        grid=(depth, n // tn),
        in_specs=[
            pl.BlockSpec((rows, d), lambda l, j: (0, 0)),
            pl.BlockSpec((1, d, tn), lambda l, j: (l, 0, j)),
            pl.BlockSpec((1, 1, tn), lambda l, j: (l, 0, j)),
        ],
        out_specs=pl.BlockSpec((1, rows, tn), lambda l, j: (l, 0, j)),
        out_shape=jax.ShapeDtypeStruct((depth, rows, n), F32),
        compiler_params=_params("parallel", "parallel"),
        name="modulation",
    )(c_all, w_mod, b_mod.reshape(depth, 1, n))
    return out.reshape(depth, rows, N_MOD, d)


def _ffn_kernel(x_ref, mod_ref, g_ref, win_ref, wout_ref, *rest, row0, d_ff, final):
    o_ref = rest[-1]
    x = x_ref[0]
    shift = mod_ref[0, row0:row0 + 1, :]
    scale = mod_ref[0, row0 + 1:row0 + 2, :]
    gate = mod_ref[0, row0 + 2:row0 + 3, :]
    h = _ada_norm(x, g_ref[...], shift, scale).astype(BF16)
    y = jnp.dot(h, win_ref[...], preferred_element_type=F32)
    g, up = y[:, :d_ff], y[:, d_ff:]
    a = (g * jax.nn.sigmoid(g) * up).astype(BF16)
    out = jnp.dot(a, wout_ref[...], preferred_element_type=F32)
    x_new = x + (0.5 * gate) * out
    if final:
        x_new = _rms(x_new) * rest[0][...]
    o_ref[0] = x_new


def _ffn(x, mods_l, mod_row, row0, g, w_in, w_out, final_g=None):
    b, n, d = x.shape
    d_ff = w_out.shape[0]
    tm = _tile(n, FFN_ROWS, V7X_SUBLANES)
    final = final_g is not None
    in_specs = [
        pl.BlockSpec((1, tm, d), lambda bi, i: (bi, i, 0)),
        pl.BlockSpec((1, N_MOD, d), lambda bi, i: (mod_row(bi), 0, 0)),
        _resident((1, d)),
        _resident((d, 2 * d_ff)),
        _resident((d_ff, d)),
    ]
    args = [x, mods_l, g.reshape(1, d), w_in, w_out]
    if final:
        in_specs.append(_resident((1, d)))
        args.append(final_g.reshape(1, d))
    return pl.pallas_call(
        functools.partial(_ffn_kernel, row0=row0, d_ff=d_ff, final=final),
        grid=(b, n // tm),
        in_specs=in_specs,
        out_specs=pl.BlockSpec((1, tm, d), lambda bi, i: (bi, i, 0)),
        out_shape=jax.ShapeDtypeStruct((b, n, d), F32),
        compiler_params=_params("parallel", "parallel"),
        name="ffn",
    )(*args)


def _head_norm_rope(z, g, cs, sn):
    p = ROPE_PAIRS
    ms = jnp.sum(z * z, axis=0, keepdims=True) * (1.0 / HEAD_DIM)
    zn = (z * lax.rsqrt(ms + EPS)) * g
    a1, a2, b1, b2 = zn[0:p], zn[p:2 * p], zn[2 * p:3 * p], zn[3 * p:4 * p]
    cr, cc = cs[0:p], cs[p:2 * p]
    sr, sc = sn[0:p], sn[p:2 * p]
    return jnp.concatenate(
        [a1 * cr - a2 * sr, a2 * cr + a1 * sr, b1 * cc - b2 * sc, b2 * cc + b1 * sc], axis=0)


def _inproj_kernel(x_ref, mod_ref, g_ref, wt_ref, wn_ref, qg_ref, kg_ref, cos_ref, sin_ref,
                   qt_ref, k_ref, vt_ref, yn_ref, *, n_heads, n_kv):
    x = x_ref[0]
    shift = mod_ref[0, 3:4, :]
    scale = mod_ref[0, 4:5, :]
    h = _ada_norm(x, g_ref[...], shift, scale).astype(BF16)
    yn_ref[0] = jnp.dot(h, wn_ref[...], preferred_element_type=F32)
    yt = lax.dot_general(wt_ref[...], h, (((1,), (1,)), ((), ())), preferred_element_type=F32)
    cs, sn = cos_ref[...], sin_ref[...]
    tm = x.shape[0]
    kv_rows = n_kv * HEAD_DIM
    for hd in range(n_heads):
        q = _head_norm_rope(yt[hd * HEAD_DIM:(hd + 1) * HEAD_DIM], qg_ref[...], cs, sn)
        q = (q * Q_SCALE).astype(BF16)
        grp = hd // GQA_GROUP
        for other in range(n_kv):
            rows = slice(other * HEAD_DIM, (other + 1) * HEAD_DIM)
            qt_ref[0, hd, rows, :] = q if other == grp else jnp.zeros_like(q)
    k0 = n_heads * HEAD_DIM
    kt = jnp.concatenate(
        [_head_norm_rope(yt[k0 + j * HEAD_DIM:k0 + (j + 1) * HEAD_DIM], kg_ref[...], cs, sn)
         for j in range(n_kv)], axis=0)
    k_ref[0] = kt.T.astype(BF16)
    v0 = k0 + kv_rows
    ones_row = (lax.broadcasted_iota(jnp.int32, (V7X_BF16_ROWS, tm), 0) == 0).astype(BF16)
    for j in range(n_kv):
        vt_ref[0, j, 0:HEAD_DIM, :] = yt[v0 + j * HEAD_DIM:v0 + (j + 1) * HEAD_DIM].astype(BF16)
        vt_ref[0, j, HEAD_DIM:HEAD_DIM + V7X_BF16_ROWS, :] = ones_row


def _inproj(x, mods_l, mod_row, g, w_t, w_n, q_g, k_g, cos_t, sin_t, n_heads, n_kv):
    b, n, d = x.shape
    n_t, n_n = w_t.shape[0], w_n.shape[1]
    kv_rows = n_kv * HEAD_DIM
    v_rows = HEAD_DIM + V7X_BF16_ROWS
    tm = _tile(n, PROJ_ROWS, V7X_LANES)
    qg_t = jnp.broadcast_to(q_g[:, None], (HEAD_DIM, tm))
    kg_t = jnp.broadcast_to(k_g[:, None], (HEAD_DIM, tm))
    return pl.pallas_call(
        functools.partial(_inproj_kernel, n_heads=n_heads, n_kv=n_kv),
        grid=(b, n // tm),
        in_specs=[
            pl.BlockSpec((1, tm, d), lambda bi, i: (bi, i, 0)),
            pl.BlockSpec((1, N_MOD, d), lambda bi, i: (mod_row(bi), 0, 0)),
            _resident((1, d)),
            _resident((n_t, d)),
            _resident((d, n_n)),
            _resident((HEAD_DIM, tm)),
            _resident((HEAD_DIM, tm)),
            pl.BlockSpec((2 * ROPE_PAIRS, tm), lambda bi, i: (0, i)),
            pl.BlockSpec((2 * ROPE_PAIRS, tm), lambda bi, i: (0, i)),
        ],
        out_specs=[
            pl.BlockSpec((1, n_heads, kv_rows, tm), lambda bi, i: (bi, 0, 0, i)),
            pl.BlockSpec((1, tm, kv_rows), lambda bi, i: (bi, i, 0)),
            pl.BlockSpec((1, n_kv, v_rows, tm), lambda bi, i: (bi, 0, 0, i)),
            pl.BlockSpec((1, tm, n_n), lambda bi, i: (bi, i, 0)),
        ],
        out_shape=[
            jax.ShapeDtypeStruct((b, n_heads, kv_rows, n), BF16),
            jax.ShapeDtypeStruct((b, n, kv_rows), BF16),
            jax.ShapeDtypeStruct((b, n_kv, v_rows, n), BF16),
            jax.ShapeDtypeStruct((b, n, n_n), F32),
        ],
        compiler_params=_params("parallel", "parallel"),
        name="inproj",
    )(x, mods_l, g.reshape(1, d), w_t, w_n, qg_t, kg_t, cos_t, sin_t)


def _attn_kernel(qt_ref, k_ref, vt_ref, o_ref, m_ref, acc_ref):
    j = pl.program_id(3)

    @pl.when(j == 0)
    def _():
        m_ref[...] = jnp.full(m_ref.shape, -jnp.inf, F32)
        acc_ref[...] = jnp.zeros(acc_ref.shape, F32)

    k = k_ref[0]
    vt = vt_ref[0, 0]
    for hd in range(GQA_GROUP):
        s = jnp.dot(k, qt_ref[0, hd], preferred_element_type=F32)
        m_prev = m_ref[hd]
        m_new = jnp.maximum(m_prev, jnp.max(s, axis=0, keepdims=True))
        p = jnp.exp2(s - m_new).astype(BF16)
        acc_ref[hd] = acc_ref[hd] * jnp.exp2(m_prev - m_new) + jnp.dot(vt, p, preferred_element_type=F32)
        m_ref[hd] = m_new

    @pl.when(j == pl.num_programs(3) - 1)
    def _():
        outs = []
        for hd in range(GQA_GROUP):
            acc = acc_ref[hd]
            outs.append(acc[0:HEAD_DIM] / acc[HEAD_DIM:HEAD_DIM + 1])
        o_ref[0] = jnp.concatenate(outs, axis=0).T


def _attention(qt, k, vt):
    b, n_heads, kv_rows, lq = qt.shape
    lk = k.shape[1]
    n_kv, v_rows = vt.shape[1], vt.shape[2]
    tq = _tile(lq, ATTN_Q_ROWS, V7X_LANES)
    tk = _tile(lk, ATTN_K_ROWS, V7X_LANES)
    grp_w = GQA_GROUP * HEAD_DIM
    return pl.pallas_call(
        _attn_kernel,
        grid=(b, n_kv, lq // tq, lk // tk),
        in_specs=[
            pl.BlockSpec((1, GQA_GROUP, kv_rows, tq), lambda bi, g, i, j: (bi, g, 0, i)),
            pl.BlockSpec((1, tk, kv_rows), lambda bi, g, i, j: (bi, j, 0)),
            pl.BlockSpec((1, 1, v_rows, tk), lambda bi, g, i, j: (bi, g, 0, j)),
        ],
        out_specs=pl.BlockSpec((1, tq, grp_w), lambda bi, g, i, j: (bi, i, g)),
        out_shape=jax.ShapeDtypeStruct((b, lq, n_heads * HEAD_DIM), F32),
        scratch_shapes=[
            pltpu.VMEM((GQA_GROUP, 1, tq), F32),
            pltpu.VMEM((GQA_GROUP, v_rows, tq), F32),
        ],
        compiler_params=_params("parallel", "parallel", "parallel", "arbitrary"),
        name="attention",
    )(qt, k, vt)


def _scan_rows(a, b, reverse):
    n = a.shape[0]
    row = lax.broadcasted_iota(jnp.int32, a.shape, 0)
    d = 1
    while d < n:
        shift = n - d if reverse else d
        valid = row < n - d if reverse else row >= d
        a_s = pltpu.roll(a, shift, 0)
        b_s = pltpu.roll(b, shift, 0)
        b = jnp.where(valid, a * b_s + b, b)
        a = jnp.where(valid, a * a_s, a)
        d *= 2
    return a, b


def _lru_kernel(u_ref, prev_ref, next_ref, cw_ref, cb_ref, wa_ref, ba_ref, wx_ref, bx_ref, lam_ref,
                h0_ref, h_ref, hl_ref, xe_ref, carry_ref, *, reverse):
    j = pl.program_id(1)
    nc = pl.num_programs(1)
    pos = nc - 1 - j if reverse else j
    tc = u_ref.shape[1]

    @pl.when(j == 0)
    def _():
        carry_ref[...] = h0_ref[0]

    xe_ref[0:HALO, :] = jnp.where(pos > 0, prev_ref[0], 0.0)
    xe_ref[HALO:HALO + tc, :] = u_ref[0]
    xe_ref[HALO + tc:, :] = jnp.where(pos < nc - 1, next_ref[0], 0.0)
    uc = cb_ref[...]
    for t in range(LRU_CONV_W):
        uc = uc + cw_ref[t:t + 1, :] * xe_ref[pl.ds(HALO - LRU_PAD_LEFT + t, tc), :]

    ub = uc.astype(BF16)
    r = jax.nn.sigmoid(jnp.dot(ub, wa_ref[...], preferred_element_type=F32) + ba_ref[...])
    i = jax.nn.sigmoid(jnp.dot(ub, wx_ref[...], preferred_element_type=F32) + bx_ref[...])
    nlam = -lam_ref[...]
    softplus = jnp.maximum(nlam, 0.0) + jnp.log1p(jnp.exp(-jnp.abs(nlam)))
    log_a = (-LRU_C * r) * softplus
    a = jnp.exp(log_a)
    xin = jnp.sqrt(-jnp.tanh(log_a) * (a * a + 1.0)) * i * uc

    big_a, big_b = _scan_rows(a, xin, reverse)
    h = big_a * carry_ref[...] + big_b
    h_ref[0] = h
    h_out = h[0:1] if reverse else h[tc - 1:tc]
    carry_ref[...] = h_out
    hl_ref[0] = h_out


def _lru(yn, col, conv_w, conv_b, wa, ba, wx, bx, lam, h0, reverse):
    b, n, _ = yn.shape
    w = wa.shape[0]
    tc = _tile(n, LRU_ROWS, V7X_SUBLANES)
    nc = n // tc
    per = tc // HALO
    last_halo = n // HALO - 1

    def pos(j):
        return nc - 1 - j if reverse else j

    vec = lambda v: v.reshape(1, w)
    return pl.pallas_call(
        functools.partial(_lru_kernel, reverse=reverse),
        grid=(b, nc),
        in_specs=[
            pl.BlockSpec((1, tc, w), lambda bi, j: (bi, pos(j), col)),
            pl.BlockSpec((1, HALO, w), lambda bi, j: (bi, jnp.maximum(pos(j) * per - 1, 0), col)),
            pl.BlockSpec((1, HALO, w), lambda bi, j: (bi, jnp.minimum((pos(j) + 1) * per, last_halo), col)),
            _resident((LRU_CONV_W, w)),
            _resident((1, w)),
            _resident((w, w)),
            _resident((1, w)),
            _resident((w, w)),
            _resident((1, w)),
            _resident((1, w)),
            pl.BlockSpec((1, 1, w), lambda bi, j: (bi, 0, 0)),
        ],
        out_specs=[
            pl.BlockSpec((1, tc, w), lambda bi, j: (bi, pos(j), 0)),
            pl.BlockSpec((1, 1, w), lambda bi, j: (bi, 0, 0)),
        ],
        out_shape=[
            jax.ShapeDtypeStruct((b, n, w), F32),
            jax.ShapeDtypeStruct((b, 1, w), F32),
        ],
        scratch_shapes=[
            pltpu.VMEM((tc + 2 * HALO, w), F32),
            pltpu.VMEM((1, w), F32),
        ],
        compiler_params=_params("parallel", "arbitrary"),
        name="lru_bwd" if reverse else "lru_fwd",
    )(yn, yn, yn, conv_w, vec(conv_b), wa, vec(ba), wx, vec(bx), vec(lam), h0)


def _merge_kernel(attn_ref, hf_ref, hb_ref, gx_ref, bg_ref, cg_ref, sx_ref, cgp_ref, sxp_ref,
                  cgn_ref, sxn_ref, x_ref, mod_ref, g_ref, w_ref, cw_ref, cb_ref, o_ref, pe_ref):
    i = pl.program_id(1)
    ni = pl.num_programs(1)
    tm = x_ref.shape[1]

    pe_ref[0:HALO, :] = jnp.where(i > 0, cgp_ref[0] * sxp_ref[0], 0.0)
    pe_ref[HALO:HALO + tm, :] = cg_ref[0] * sx_ref[0]
    pe_ref[HALO + tm:, :] = jnp.where(i < ni - 1, cgn_ref[0] * sxn_ref[0], 0.0)
    conv = cb_ref[...]
    for t in range(SC_CONV_W):
        conv = conv + cw_ref[t:t + 1, :] * pe_ref[pl.ds(HALO - SC_PAD_LEFT + t, tm), :]
    sc = bg_ref[0] * conv

    lru = (hf_ref[0] + hb_ref[0]) * _gelu_tanh(gx_ref[0])
    cat = jnp.concatenate([_rms(attn_ref[0]), _rms(lru), _rms(sc)], axis=-1) * g_ref[...]
    out = jnp.dot(cat.astype(BF16), w_ref[...], preferred_element_type=F32)
    o_ref[0] = x_ref[0] + mod_ref[0, 5:6, :] * out


def _merge(attn, hf, hb, yn, x, mods_l, mod_row, grp_g, w_out, sc_w, sc_b):
    b, n, d = x.shape
    w = hf.shape[2]
    aw = attn.shape[2]
    tm = _tile(n, MERGE_ROWS, V7X_SUBLANES)
    per = tm // HALO
    last_halo = n // HALO - 1
    main = lambda col: pl.BlockSpec((1, tm, w), lambda bi, i: (bi, i, col))
    prev = lambda col: pl.BlockSpec((1, HALO, w), lambda bi, i: (bi, jnp.maximum(i * per - 1, 0), col))
    nxt = lambda col: pl.BlockSpec((1, HALO, w), lambda bi, i: (bi, jnp.minimum((i + 1) * per, last_halo), col))
    return pl.pallas_call(
        _merge_kernel,
        grid=(b, n // tm),
        in_specs=[
            pl.BlockSpec((1, tm, aw), lambda bi, i: (bi, i, 0)),
            main(0), main(0),
            main(1), main(2), main(3), main(4),
            prev(3), prev(4), nxt(3), nxt(4),
            pl.BlockSpec((1, tm, d), lambda bi, i: (bi, i, 0)),
            pl.BlockSpec((1, N_MOD, d), lambda bi, i: (mod_row(bi), 0, 0)),
            _resident((1, d)),
            _resident((d, d)),
            _resident((SC_CONV_W, w)),
            _resident((1, w)),
        ],
        out_specs=pl.BlockSpec((1, tm, d), lambda bi, i: (bi, i, 0)),
        out_shape=jax.ShapeDtypeStruct((b, n, d), F32),
        scratch_shapes=[pltpu.VMEM((tm + 2 * HALO, w), F32)],
        compiler_params=_params("parallel", "arbitrary"),
        name="merge",
    )(attn, hf, hb, yn, yn, yn, yn, yn, yn, yn, yn, x, mods_l, grp_g.reshape(1, d), w_out, sc_w,
      sc_b.reshape(1, w))


def _rope_tables_t(seq):
    rows = seq // GRID_W
    row_ids = jnp.repeat(jnp.arange(rows), GRID_W).astype(F32)
    col_ids = jnp.tile(jnp.arange(GRID_W), rows).astype(F32)
    inv_freq = ROPE_THETA ** (-jnp.arange(ROPE_PAIRS, dtype=F32) / ROPE_PAIRS)
    ang = jnp.stack([row_ids[:, None] * inv_freq, col_ids[:, None] * inv_freq], axis=1)
    ang_t = ang.reshape(seq, 2 * ROPE_PAIRS).T
    return jnp.cos(ang_t), jnp.sin(ang_t)


def _block_diag(w):
    nb, n, _ = w.shape
    eye = jnp.eye(nb, dtype=w.dtype)
    return (eye[:, None, :, None] * w[:, :, None, :]).reshape(nb * n, nb * n)


def kernel(x, c, ctx, c_ctx, w_mod, b_mod, norm_g, w_ffn_in, w_ffn_out, w_in, q_norm_g, k_norm_g,
           lru_conv_w, lru_conv_b, lru_wa, lru_ba, lru_wx, lru_bx, lru_lambda, sc_conv_w, sc_conv_b,
           grp_norm_g, w_out, final_norm_g):
    bsz, seq, d = x.shape
    n_ctx = ctx.shape[1]
    depth = w_mod.shape[0]
    attn_w = d // 2
    n_heads = attn_w // HEAD_DIM
    n_kv = n_heads // GQA_GROUP
    kv_w = n_kv * HEAD_DIM
    lru_w = lru_conv_w.shape[2]
    n_t = attn_w + 2 * kv_w

    rows = -(-(bsz + 1) // V7X_SUBLANES) * V7X_SUBLANES
    c_all = jnp.zeros((rows, d), F32).at[:bsz].set(c).at[bsz].set(c_ctx)
    mods = _modulation(c_all, w_mod, b_mod)
    x_row = lambda bi: bi
    ctx_row = lambda bi: bsz

    cos_x, sin_x = _rope_tables_t(seq)
    cos_c = jnp.ones((2 * ROPE_PAIRS, n_ctx), F32)
    sin_c = jnp.zeros((2 * ROPE_PAIRS, n_ctx), F32)
    zero_state = jnp.zeros((bsz, 1, lru_w), F32)

    h_ctx = ctx
    for l in range(depth):
        last = l == depth - 1
        wf_in = w_ffn_in[l].astype(BF16)
        wf_out = w_ffn_out[l].astype(BF16)
        w_t = w_in[l][:, :n_t].T.astype(BF16)
        w_n = w_in[l][:, n_t:].astype(BF16)
        wo = w_out[l].astype(BF16)
        lru_p = [(lru_conv_w[l], lru_conv_b[l], _block_diag(lru_wa[l, dr]).astype(BF16), lru_ba[l, dr],
                  _block_diag(lru_wx[l, dr]).astype(BF16), lru_bx[l, dr], lru_lambda[l, dr]) for dr in range(2)]

        x = _ffn(x, mods[l], x_row, 0, norm_g[l, 0], wf_in[0], wf_out[0])
        h_ctx = _ffn(h_ctx, mods[l], ctx_row, 0, norm_g[l, 0], wf_in[0], wf_out[0])

        proj = functools.partial(_inproj, g=norm_g[l, 1], w_t=w_t, w_n=w_n, q_g=q_norm_g[l], k_g=k_norm_g[l],
                                 n_heads=n_heads, n_kv=n_kv)
        qt_x, k_x, vt_x, yn_x = proj(x, mods[l], x_row, cos_t=cos_x, sin_t=sin_x)
        qt_c, k_c, vt_c, yn_c = proj(h_ctx, mods[l], ctx_row, cos_t=cos_c, sin_t=sin_c)

        attn_x = _attention(qt_x, jnp.concatenate([k_c, k_x], axis=1), jnp.concatenate([vt_c, vt_x], axis=3))

        hc_f, state_f = _lru(yn_c, 0, *lru_p[0], zero_state, reverse=False)
        hc_b, state_b = _lru(yn_c, 0, *lru_p[1], zero_state, reverse=True)
        hx_f, _ = _lru(yn_x, 0, *lru_p[0], state_f, reverse=False)
        hx_b, _ = _lru(yn_x, 0, *lru_p[1], state_b, reverse=True)

        x = _merge(attn_x, hx_f, hx_b, yn_x, x, mods[l], x_row, grp_norm_g[l], wo, sc_conv_w[l], sc_conv_b[l])
        x = _ffn(x, mods[l], x_row, 6, norm_g[l, 2], wf_in[1], wf_out[1],
                 final_g=final_norm_g if last else None)
        if not last:
            attn_c = _attention(qt_c, k_c, vt_c)
            h_ctx = _merge(attn_c, hc_f, hc_b, yn_c, h_ctx, mods[l], ctx_row, grp_norm_g[l], wo,
                           sc_conv_w[l], sc_conv_b[l])
            h_ctx = _ffn(h_ctx, mods[l], ctx_row, 6, norm_g[l, 2], wf_in[1], wf_out[1])
    return x
```

```python
import functools
import math

import jax
import jax.numpy as jnp
from jax import lax
from jax.experimental import pallas as pl
from jax.experimental.pallas import tpu as pltpu

F32 = jnp.float32
BF16 = jnp.bfloat16

HEAD_DIM = 64
GQA_GROUP = 4
ROPE_PAIRS = HEAD_DIM // 4
GRID_W = 64
ROPE_THETA = 10000.0
N_MOD = 9
EPS = 1e-6
LRU_C = 8.0
LRU_CONV_W = 4
LRU_PAD_LEFT = 2
SC_CONV_W = 3
SC_PAD_LEFT = 1
Q_SCALE = HEAD_DIM ** -0.5 * math.log2(math.e)
P_SUM_LIMIT = 2.0 ** 64
M_SAMPLE_KEYS = 128

V7X_SUBLANES = 8
V7X_LANES = 128
V7X_BF16_ROWS = 16
V7X_VMEM_BYTES = 64 * 2 ** 20
VMEM_LIMIT_BYTES = V7X_VMEM_BYTES * 7 // 8

FFN_ROWS = 256
PROJ_ROWS = 512
ATTN_K_ROWS = 1408
LRU_ROWS = 512
MERGE_ROWS = 512
HALO = V7X_SUBLANES


def _tile(length, target, align):
    best = None
    for t in range(align, min(length, target) + 1, align):
        if length % t == 0:
            best = t
    if best is None:
        raise ValueError(f"no {align}-aligned tile divides {length}")
    return best


def _params(*semantics):
    return pltpu.CompilerParams(dimension_semantics=semantics, vmem_limit_bytes=VMEM_LIMIT_BYTES)


def _resident(shape):
    zeros = (0,) * len(shape)
    return pl.BlockSpec(shape, lambda *_: zeros, pipeline_mode=pl.Buffered(1))


def _rms(t):
    return t * lax.rsqrt(jnp.mean(t * t, axis=-1, keepdims=True) + EPS)


def _ada_norm(x, g, shift, scale):
    return (_rms(x) * g) * (1.0 + scale) + shift


def _gelu_tanh(x):
    cdf = 0.5 * (1.0 + jnp.tanh(math.sqrt(2.0 / math.pi) * (x + 0.044715 * (x * x * x))))
    return x * cdf


def _mod_kernel(c_ref, w_ref, b_ref, o_ref):
    c = c_ref[...]
    a = (c * jax.nn.sigmoid(c)).astype(BF16)
    o_ref[0] = jnp.dot(a, w_ref[0].astype(BF16), preferred_element_type=F32) + b_ref[0]


def _modulation(c_all, w_mod, b_mod):
    depth, d, n = w_mod.shape
    rows = c_all.shape[0]
    tn = _tile(n, d, V7X_LANES)
    out = pl.pallas_call(
        _mod_kernel,
        grid=(depth, n // tn),
        in_specs=[
            pl.BlockSpec((rows, d), lambda l, j: (0, 0)),
            pl.BlockSpec((1, d, tn), lambda l, j: (l, 0, j)),
            pl.BlockSpec((1, 1, tn), lambda l, j: (l, 0, j)),
        ],
        out_specs=pl.BlockSpec((1, rows, tn), lambda l, j: (l, 0, j)),
        out_shape=jax.ShapeDtypeStruct((depth, rows, n), F32),
        compiler_params=_params("parallel", "parallel"),
        name="modulation",
    )(c_all, w_mod, b_mod.reshape(depth, 1, n))
    return out.reshape(depth, rows, N_MOD, d)


def _ffn_kernel(x_ref, mod_ref, g_ref, win_ref, wout_ref, *rest, row0, d_ff, final):
    o_ref = rest[-1]
    x = x_ref[0]
    shift = mod_ref[0, row0:row0 + 1, :]
    scale = mod_ref[0, row0 + 1:row0 + 2, :]
    gate = mod_ref[0, row0 + 2:row0 + 3, :]
    h = _ada_norm(x, g_ref[...], shift, scale).astype(BF16)
    y = jnp.dot(h, win_ref[...], preferred_element_type=F32)
    g, up = y[:, :d_ff], y[:, d_ff:]
    a = (g * jax.nn.sigmoid(g) * up).astype(BF16)
    out = jnp.dot(a, wout_ref[...], preferred_element_type=F32)
    x_new = x + (0.5 * gate) * out
    if final:
        x_new = _rms(x_new) * rest[0][...]
    o_ref[0] = x_new


def _ffn(x, mods_l, mod_row, row0, g, w_in, w_out, final_g=None):
    b, n, d = x.shape
    d_ff = w_out.shape[0]
    tm = _tile(n, FFN_ROWS, V7X_SUBLANES)
    final = final_g is not None
    in_specs = [
        pl.BlockSpec((1, tm, d), lambda bi, i: (bi, i, 0)),
        pl.BlockSpec((1, N_MOD, d), lambda bi, i: (mod_row(bi), 0, 0)),
        _resident((1, d)),
        _resident((d, 2 * d_ff)),
        _resident((d_ff, d)),
    ]
    args = [x, mods_l, g.reshape(1, d), w_in, w_out]
    if final:
        in_specs.append(_resident((1, d)))
        args.append(final_g.reshape(1, d))
    return pl.pallas_call(
        functools.partial(_ffn_kernel, row0=row0, d_ff=d_ff, final=final),
        grid=(b, n // tm),
        in_specs=in_specs,
        out_specs=pl.BlockSpec((1, tm, d), lambda bi, i: (bi, i, 0)),
        out_shape=jax.ShapeDtypeStruct((b, n, d), F32),
        compiler_params=_params("parallel", "parallel"),
        name="ffn",
    )(*args)


def _head_norm_rope(z, g, cs, sn):
    p = ROPE_PAIRS
    ms = jnp.sum(z * z, axis=0, keepdims=True) * (1.0 / HEAD_DIM)
    zn = (z * lax.rsqrt(ms + EPS)) * g
    a1, a2, b1, b2 = zn[0:p], zn[p:2 * p], zn[2 * p:3 * p], zn[3 * p:4 * p]
    cr, cc = cs[0:p], cs[p:2 * p]
    sr, sc = sn[0:p], sn[p:2 * p]
    return jnp.concatenate(
        [a1 * cr - a2 * sr, a2 * cr + a1 * sr, b1 * cc - b2 * sc, b2 * cc + b1 * sc], axis=0)


def _inproj_kernel(x_ref, mod_ref, g_ref, wt_ref, wn_ref, qg_ref, kg_ref, cos_ref, sin_ref,
                   qt_ref, k_ref, vt_ref, yn_ref, *, n_heads, n_kv):
    x = x_ref[0]
    shift = mod_ref[0, 3:4, :]
    scale = mod_ref[0, 4:5, :]
    h = _ada_norm(x, g_ref[...], shift, scale).astype(BF16)
    yn_ref[0] = jnp.dot(h, wn_ref[...], preferred_element_type=F32)
    yt = lax.dot_general(wt_ref[...], h, (((1,), (1,)), ((), ())), preferred_element_type=F32)
    cs, sn = cos_ref[...], sin_ref[...]
    tm = x.shape[0]
    kv_rows = n_kv * HEAD_DIM
    for hd in range(n_heads):
        q = _head_norm_rope(yt[hd * HEAD_DIM:(hd + 1) * HEAD_DIM], qg_ref[...], cs, sn)
        q = (q * Q_SCALE).astype(BF16)
        grp, member = divmod(hd, GQA_GROUP)
        cols = slice(member * tm, (member + 1) * tm)
        for other in range(n_kv):
            rows = slice(other * HEAD_DIM, (other + 1) * HEAD_DIM)
            qt_ref[0, grp, rows, cols] = q if other == grp else jnp.zeros_like(q)
    k0 = n_heads * HEAD_DIM
    kt = jnp.concatenate(
        [_head_norm_rope(yt[k0 + j * HEAD_DIM:k0 + (j + 1) * HEAD_DIM], kg_ref[...], cs, sn)
         for j in range(n_kv)], axis=0)
    k_ref[0] = kt.T.astype(BF16)
    v0 = k0 + kv_rows
    ones_row = (lax.broadcasted_iota(jnp.int32, (V7X_BF16_ROWS, tm), 0) == 0).astype(BF16)
    for j in range(n_kv):
        vt_ref[0, j, 0:HEAD_DIM, :] = yt[v0 + j * HEAD_DIM:v0 + (j + 1) * HEAD_DIM].astype(BF16)
        vt_ref[0, j, HEAD_DIM:HEAD_DIM + V7X_BF16_ROWS, :] = ones_row


def _inproj(x, mods_l, mod_row, g, w_t, w_n, q_g, k_g, cos_t, sin_t, n_heads, n_kv):
    b, n, d = x.shape
    n_t, n_n = w_t.shape[0], w_n.shape[1]
    kv_rows = n_kv * HEAD_DIM
    v_rows = HEAD_DIM + V7X_BF16_ROWS
    tm = _tile(n, PROJ_ROWS, V7X_LANES)
    assert n_heads == n_kv * GQA_GROUP
    qg_t = jnp.broadcast_to(q_g[:, None], (HEAD_DIM, tm))
    kg_t = jnp.broadcast_to(k_g[:, None], (HEAD_DIM, tm))
    outs = pl.pallas_call(
        functools.partial(_inproj_kernel, n_heads=n_heads, n_kv=n_kv),
        grid=(b, n // tm),
        in_specs=[
            pl.BlockSpec((1, tm, d), lambda bi, i: (bi, i, 0)),
            pl.BlockSpec((1, N_MOD, d), lambda bi, i: (mod_row(bi), 0, 0)),
            _resident((1, d)),
            _resident((n_t, d)),
            _resident((d, n_n)),
            _resident((HEAD_DIM, tm)),
            _resident((HEAD_DIM, tm)),
            pl.BlockSpec((2 * ROPE_PAIRS, tm), lambda bi, i: (0, i)),
            pl.BlockSpec((2 * ROPE_PAIRS, tm), lambda bi, i: (0, i)),
        ],
        out_specs=[
            pl.BlockSpec((1, n_kv, kv_rows, GQA_GROUP * tm), lambda bi, i: (bi, 0, 0, i)),
            pl.BlockSpec((1, tm, kv_rows), lambda bi, i: (bi, i, 0)),
            pl.BlockSpec((1, n_kv, v_rows, tm), lambda bi, i: (bi, 0, 0, i)),
            pl.BlockSpec((1, tm, n_n), lambda bi, i: (bi, i, 0)),
        ],
        out_shape=[
            jax.ShapeDtypeStruct((b, n_kv, kv_rows, GQA_GROUP * n), BF16),
            jax.ShapeDtypeStruct((b, n, kv_rows), BF16),
            jax.ShapeDtypeStruct((b, n_kv, v_rows, n), BF16),
            jax.ShapeDtypeStruct((b, n, n_n), F32),
        ],
        compiler_params=_params("parallel", "parallel"),
        name="inproj",
    )(x, mods_l, g.reshape(1, d), w_t, w_n, qg_t, kg_t, cos_t, sin_t)
    return (*outs, tm)


def _attn_kernel(qt_ref, k_ref, vt_ref, o_ref, m_ref, acc_ref):
    j = pl.program_id(3)
    tq = o_ref.shape[1]
    qt = qt_ref[0, 0]
    sample = min(M_SAMPLE_KEYS, k_ref.shape[1])

    @pl.when(j == 0)
    def _():
        s0 = jnp.dot(k_ref[0, 0:sample, :], qt, preferred_element_type=F32)
        m_ref[...] = jnp.max(s0, axis=0, keepdims=True)
        acc_ref[...] = jnp.zeros(acc_ref.shape, F32)

    s = jnp.dot(k_ref[0], qt, preferred_element_type=F32)
    p = jnp.exp2(s - m_ref[...]).astype(BF16)
    pv = jnp.dot(vt_ref[0, 0], p, preferred_element_type=F32)
    safe = jnp.max(pv[HEAD_DIM:HEAD_DIM + 1]) <= P_SUM_LIMIT

    @pl.when(safe)
    def _():
        acc_ref[...] += pv

    @pl.when(jnp.logical_not(safe))
    def _():
        s2 = jnp.dot(k_ref[0], qt, preferred_element_type=F32)
        m_prev = m_ref[...]
        m_new = jnp.maximum(m_prev, jnp.max(s2, axis=0, keepdims=True))
        p2 = jnp.exp2(s2 - m_new).astype(BF16)
        acc_ref[...] = (acc_ref[...] * jnp.exp2(m_prev - m_new)
                        + jnp.dot(vt_ref[0, 0], p2, preferred_element_type=F32))
        m_ref[...] = m_new

    @pl.when(j == pl.num_programs(3) - 1)
    def _():
        acc = acc_ref[...]
        o = acc[0:HEAD_DIM] / acc[HEAD_DIM:HEAD_DIM + 1]
        o = jnp.concatenate([o[:, h * tq:(h + 1) * tq] for h in range(GQA_GROUP)], axis=0)
        o_ref[0] = o.T


def _attention(qt, k, vt, tq):
    b, n_kv, kv_rows, lq_g = qt.shape
    lq = lq_g // GQA_GROUP
    lk = k.shape[1]
    v_rows = vt.shape[2]
    tk = _tile(lk, ATTN_K_ROWS, V7X_LANES)
    grp_w = GQA_GROUP * HEAD_DIM
    return pl.pallas_call(
        _attn_kernel,
        grid=(b, n_kv, lq // tq, lk // tk),
        in_specs=[
            pl.BlockSpec((1, 1, kv_rows, GQA_GROUP * tq), lambda bi, g, i, j: (bi, g, 0, i)),
            pl.BlockSpec((1, tk, kv_rows), lambda bi, g, i, j: (bi, j, 0)),
            pl.BlockSpec((1, 1, v_rows, tk), lambda bi, g, i, j: (bi, g, 0, j)),
        ],
        out_specs=pl.BlockSpec((1, tq, grp_w), lambda bi, g, i, j: (bi, i, g)),
        out_shape=jax.ShapeDtypeStruct((b, lq, n_kv * grp_w), F32),
        scratch_shapes=[
            pltpu.VMEM((1, GQA_GROUP * tq), F32),
            pltpu.VMEM((v_rows, GQA_GROUP * tq), F32),
        ],
        compiler_params=_params("parallel", "parallel", "parallel", "arbitrary"),
        name="attention",
    )(qt, k, vt)


def _scan_rows(a, b, reverse):
    n = a.shape[0]
    row = lax.broadcasted_iota(jnp.int32, a.shape, 0)
    d = 1
    while d < n:
        shift = n - d if reverse else d
        valid = row < n - d if reverse else row >= d
        a_s = pltpu.roll(a, shift, 0)
        b_s = pltpu.roll(b, shift, 0)
        b = jnp.where(valid, a * b_s + b, b)
        a = jnp.where(valid, a * a_s, a)
        d *= 2
    return a, b


def _lru_kernel(u_ref, prev_ref, next_ref, cw_ref, cb_ref, wa_ref, ba_ref, wx_ref, bx_ref, lam_ref,
                h0_ref, h_ref, hl_ref, xe_ref, carry_ref, *, reverse):
    j = pl.program_id(1)
    nc = pl.num_programs(1)
    pos = nc - 1 - j if reverse else j
    tc = u_ref.shape[1]

    @pl.when(j == 0)
    def _():
        carry_ref[...] = h0_ref[0]

    xe_ref[0:HALO, :] = jnp.where(pos > 0, prev_ref[0], 0.0)
    xe_ref[HALO:HALO + tc, :] = u_ref[0]
    xe_ref[HALO + tc:, :] = jnp.where(pos < nc - 1, next_ref[0], 0.0)
    uc = cb_ref[...]
    for t in range(LRU_CONV_W):
        uc = uc + cw_ref[t:t + 1, :] * xe_ref[pl.ds(HALO - LRU_PAD_LEFT + t, tc), :]

    ub = uc.astype(BF16)
    r = jax.nn.sigmoid(jnp.dot(ub, wa_ref[...], preferred_element_type=F32) + ba_ref[...])
    i = jax.nn.sigmoid(jnp.dot(ub, wx_ref[...], preferred_element_type=F32) + bx_ref[...])
    nlam = -lam_ref[...]
    softplus = jnp.maximum(nlam, 0.0) + jnp.log1p(jnp.exp(-jnp.abs(nlam)))
    log_a = (-LRU_C * r) * softplus
    a = jnp.exp(log_a)
    xin = jnp.sqrt(-jnp.tanh(log_a) * (a * a + 1.0)) * i * uc

    big_a, big_b = _scan_rows(a, xin, reverse)
    h = big_a * carry_ref[...] + big_b
    h_ref[0] = h
    h_out = h[0:1] if reverse else h[tc - 1:tc]
    carry_ref[...] = h_out
    hl_ref[0] = h_out


def _lru(yn, col, conv_w, conv_b, wa, ba, wx, bx, lam, h0, reverse):
    b, n, _ = yn.shape
    w = wa.shape[0]
    tc = _tile(n, LRU_ROWS, V7X_SUBLANES)
    nc = n // tc
    per = tc // HALO
    last_halo = n // HALO - 1

    def pos(j):
        return nc - 1 - j if reverse else j

    vec = lambda v: v.reshape(1, w)
    return pl.pallas_call(
        functools.partial(_lru_kernel, reverse=reverse),
        grid=(b, nc),
        in_specs=[
            pl.BlockSpec((1, tc, w), lambda bi, j: (bi, pos(j), col)),
            pl.BlockSpec((1, HALO, w), lambda bi, j: (bi, jnp.maximum(pos(j) * per - 1, 0), col)),
            pl.BlockSpec((1, HALO, w), lambda bi, j: (bi, jnp.minimum((pos(j) + 1) * per, last_halo), col)),
            _resident((LRU_CONV_W, w)),
            _resident((1, w)),
            _resident((w, w)),
            _resident((1, w)),
            _resident((w, w)),
            _resident((1, w)),
            _resident((1, w)),
            pl.BlockSpec((1, 1, w), lambda bi, j: (bi, 0, 0)),
        ],
        out_specs=[
            pl.BlockSpec((1, tc, w), lambda bi, j: (bi, pos(j), 0)),
            pl.BlockSpec((1, 1, w), lambda bi, j: (bi, 0, 0)),
        ],
        out_shape=[
            jax.ShapeDtypeStruct((b, n, w), F32),
            jax.ShapeDtypeStruct((b, 1, w), F32),
        ],
        scratch_shapes=[
            pltpu.VMEM((tc + 2 * HALO, w), F32),
            pltpu.VMEM((1, w), F32),
        ],
        compiler_params=_params("parallel", "arbitrary"),
        name="lru_bwd" if reverse else "lru_fwd",
    )(yn, yn, yn, conv_w, vec(conv_b), wa, vec(ba), wx, vec(bx), vec(lam), h0)


def _merge_kernel(attn_ref, hf_ref, hb_ref, gx_ref, bg_ref, cg_ref, sx_ref, cgp_ref, sxp_ref,
                  cgn_ref, sxn_ref, x_ref, mod_ref, g_ref, w_ref, cw_ref, cb_ref, o_ref, pe_ref):
    i = pl.program_id(1)
    ni = pl.num_programs(1)
    tm = x_ref.shape[1]

    pe_ref[0:HALO, :] = jnp.where(i > 0, cgp_ref[0] * sxp_ref[0], 0.0)
    pe_ref[HALO:HALO + tm, :] = cg_ref[0] * sx_ref[0]
    pe_ref[HALO + tm:, :] = jnp.where(i < ni - 1, cgn_ref[0] * sxn_ref[0], 0.0)
    conv = cb_ref[...]
    for t in range(SC_CONV_W):
        conv = conv + cw_ref[t:t + 1, :] * pe_ref[pl.ds(HALO - SC_PAD_LEFT + t, tm), :]
    sc = bg_ref[0] * conv

    lru = (hf_ref[0] + hb_ref[0]) * _gelu_tanh(gx_ref[0])
    cat = jnp.concatenate([_rms(attn_ref[0]), _rms(lru), _rms(sc)], axis=-1) * g_ref[...]
    out = jnp.dot(cat.astype(BF16), w_ref[...], preferred_element_type=F32)
    o_ref[0] = x_ref[0] + mod_ref[0, 5:6, :] * out


def _merge(attn, hf, hb, yn, x, mods_l, mod_row, grp_g, w_out, sc_w, sc_b):
    b, n, d = x.shape
    w = hf.shape[2]
    aw = attn.shape[2]
    tm = _tile(n, MERGE_ROWS, V7X_SUBLANES)
    per = tm // HALO
    last_halo = n // HALO - 1
    main = lambda col: pl.BlockSpec((1, tm, w), lambda bi, i: (bi, i, col))
    prev = lambda col: pl.BlockSpec((1, HALO, w), lambda bi, i: (bi, jnp.maximum(i * per - 1, 0), col))
    nxt = lambda col: pl.BlockSpec((1, HALO, w), lambda bi, i: (bi, jnp.minimum((i + 1) * per, last_halo), col))
    return pl.pallas_call(
        _merge_kernel,
        grid=(b, n // tm),
        in_specs=[
            pl.BlockSpec((1, tm, aw), lambda bi, i: (bi, i, 0)),
            main(0), main(0),
            main(1), main(2), main(3), main(4),
            prev(3), prev(4), nxt(3), nxt(4),
            pl.BlockSpec((1, tm, d), lambda bi, i: (bi, i, 0)),
            pl.BlockSpec((1, N_MOD, d), lambda bi, i: (mod_row(bi), 0, 0)),
            _resident((1, d)),
            _resident((d, d)),
            _resident((SC_CONV_W, w)),
            _resident((1, w)),
        ],
        out_specs=pl.BlockSpec((1, tm, d), lambda bi, i: (bi, i, 0)),
        out_shape=jax.ShapeDtypeStruct((b, n, d), F32),
        scratch_shapes=[pltpu.VMEM((tm + 2 * HALO, w), F32)],
        compiler_params=_params("parallel", "arbitrary"),
        name="merge",
    )(attn, hf, hb, yn, yn, yn, yn, yn, yn, yn, yn, x, mods_l, grp_g.reshape(1, d), w_out, sc_w,
      sc_b.reshape(1, w))


def _rope_tables_t(seq):
    rows = seq // GRID_W
    row_ids = jnp.repeat(jnp.arange(rows), GRID_W).astype(F32)
    col_ids = jnp.tile(jnp.arange(GRID_W), rows).astype(F32)
    inv_freq = ROPE_THETA ** (-jnp.arange(ROPE_PAIRS, dtype=F32) / ROPE_PAIRS)
    ang = jnp.stack([row_ids[:, None] * inv_freq, col_ids[:, None] * inv_freq], axis=1)
    ang_t = ang.reshape(seq, 2 * ROPE_PAIRS).T
    return jnp.cos(ang_t), jnp.sin(ang_t)


def _block_diag(w):
    nb, n, _ = w.shape
    eye = jnp.eye(nb, dtype=w.dtype)
    return (eye[:, None, :, None] * w[:, :, None, :]).reshape(nb * n, nb * n)


def kernel(x, c, ctx, c_ctx, w_mod, b_mod, norm_g, w_ffn_in, w_ffn_out, w_in, q_norm_g, k_norm_g,
           lru_conv_w, lru_conv_b, lru_wa, lru_ba, lru_wx, lru_bx, lru_lambda, sc_conv_w, sc_conv_b,
           grp_norm_g, w_out, final_norm_g):
    bsz, seq, d = x.shape
    n_ctx = ctx.shape[1]
    depth = w_mod.shape[0]
    attn_w = d // 2
    n_heads = attn_w // HEAD_DIM
    n_kv = n_heads // GQA_GROUP
    kv_w = n_kv * HEAD_DIM
    lru_w = lru_conv_w.shape[2]
    n_t = attn_w + 2 * kv_w

    rows = -(-(bsz + 1) // V7X_SUBLANES) * V7X_SUBLANES
    c_all = jnp.zeros((rows, d), F32).at[:bsz].set(c).at[bsz].set(c_ctx)
    mods = _modulation(c_all, w_mod, b_mod)
    x_row = lambda bi: bi
    ctx_row = lambda bi: bsz

    cos_x, sin_x = _rope_tables_t(seq)
    cos_c = jnp.ones((2 * ROPE_PAIRS, n_ctx), F32)
    sin_c = jnp.zeros((2 * ROPE_PAIRS, n_ctx), F32)
    zero_state = jnp.zeros((bsz, 1, lru_w), F32)

    h_ctx = ctx
    for l in range(depth):
        last = l == depth - 1
        wf_in = w_ffn_in[l].astype(BF16)
        wf_out = w_ffn_out[l].astype(BF16)
        w_t = w_in[l][:, :n_t].T.astype(BF16)
        w_n = w_in[l][:, n_t:].astype(BF16)
        wo = w_out[l].astype(BF16)
        lru_p = [(lru_conv_w[l], lru_conv_b[l], _block_diag(lru_wa[l, dr]).astype(BF16), lru_ba[l, dr],
                  _block_diag(lru_wx[l, dr]).astype(BF16), lru_bx[l, dr], lru_lambda[l, dr]) for dr in range(2)]

        x = _ffn(x, mods[l], x_row, 0, norm_g[l, 0], wf_in[0], wf_out[0])
        h_ctx = _ffn(h_ctx, mods[l], ctx_row, 0, norm_g[l, 0], wf_in[0], wf_out[0])

        proj = functools.partial(_inproj, g=norm_g[l, 1], w_t=w_t, w_n=w_n, q_g=q_norm_g[l], k_g=k_norm_g[l],
                                 n_heads=n_heads, n_kv=n_kv)
        qt_x, k_x, vt_x, yn_x, tq_x = proj(x, mods[l], x_row, cos_t=cos_x, sin_t=sin_x)
        qt_c, k_c, vt_c, yn_c, tq_c = proj(h_ctx, mods[l], ctx_row, cos_t=cos_c, sin_t=sin_c)

        attn_x = _attention(qt_x, jnp.concatenate([k_c, k_x], axis=1), jnp.concatenate([vt_c, vt_x], axis=3),
                            tq_x)

        hc_f, state_f = _lru(yn_c, 0, *lru_p[0], zero_state, reverse=False)
        hc_b, state_b = _lru(yn_c, 0, *lru_p[1], zero_state, reverse=True)
        hx_f, _ = _lru(yn_x, 0, *lru_p[0], state_f, reverse=False)
        hx_b, _ = _lru(yn_x, 0, *lru_p[1], state_b, reverse=True)

        x = _merge(attn_x, hx_f, hx_b, yn_x, x, mods[l], x_row, grp_norm_g[l], wo, sc_conv_w[l], sc_conv_b[l])
        x = _ffn(x, mods[l], x_row, 6, norm_g[l, 2], wf_in[1], wf_out[1],
                 final_g=final_norm_g if last else None)
        if not last:
            attn_c = _attention(qt_c, k_c, vt_c, tq_c)
            h_ctx = _merge(attn_c, hc_f, hc_b, yn_c, h_ctx, mods[l], ctx_row, grp_norm_g[l], wo,
                           sc_conv_w[l], sc_conv_b[l])
            h_ctx = _ffn(h_ctx, mods[l], ctx_row, 6, norm_g[l, 2], wf_in[1], wf_out[1])
    return x
```

```python
import functools
import math

import jax
import jax.numpy as jnp
from jax import lax
from jax.experimental import pallas as pl
from jax.experimental.pallas import tpu as pltpu

F32 = jnp.float32
BF16 = jnp.bfloat16

HEAD_DIM = 64
GQA_GROUP = 4
ROPE_PAIRS = HEAD_DIM // 4
GRID_W = 64
ROPE_THETA = 10000.0
N_MOD = 9
EPS = 1e-6
LRU_C = 8.0
LRU_CONV_W = 4
LRU_PAD_LEFT = 2
SC_CONV_W = 3
SC_PAD_LEFT = 1
Q_SCALE = HEAD_DIM ** -0.5 * math.log2(math.e)
P_SUM_LIMIT = 2.0 ** 64
M_SAMPLE_KEYS = 128

V7X_SUBLANES = 8
V7X_LANES = 128
V7X_BF16_ROWS = 16
V7X_VMEM_BYTES = 64 * 2 ** 20
VMEM_LIMIT_BYTES = V7X_VMEM_BYTES * 7 // 8

FFN_ROWS = 512
PROJ_ROWS = 512
ATTN_K_ROWS = 2816
LRU_ROWS = 512
MERGE_ROWS = 512
HALO = V7X_SUBLANES


def _tile(length, target, align):
    best = None
    for t in range(align, min(length, target) + 1, align):
        if length % t == 0:
            best = t
    if best is None:
        raise ValueError(f"no {align}-aligned tile divides {length}")
    return best


def _params(*semantics):
    return pltpu.CompilerParams(dimension_semantics=semantics, vmem_limit_bytes=VMEM_LIMIT_BYTES)


def _resident(shape):
    zeros = (0,) * len(shape)
    return pl.BlockSpec(shape, lambda *_: zeros, pipeline_mode=pl.Buffered(1))


def _rms(t):
    return t * lax.rsqrt(jnp.mean(t * t, axis=-1, keepdims=True) + EPS)


def _ada_norm(x, g, shift, scale):
    return (_rms(x) * g) * (1.0 + scale) + shift


def _gelu_tanh(x):
    cdf = 0.5 * (1.0 + jnp.tanh(math.sqrt(2.0 / math.pi) * (x + 0.044715 * (x * x * x))))
    return x * cdf


def _mod_kernel(c_ref, w_ref, b_ref, o_ref):
    c = c_ref[...]
    a = (c * jax.nn.sigmoid(c)).astype(BF16)
    o_ref[0] = jnp.dot(a, w_ref[0].astype(BF16), preferred_element_type=F32) + b_ref[0]


def _modulation(c_all, w_mod, b_mod):
    depth, d, n = w_mod.shape
    rows = c_all.shape[0]
    tn = _tile(n, d, V7X_LANES)
    out = pl.pallas_call(
        _mod_kernel,
        grid=(depth, n // tn),
        in_specs=[
            pl.BlockSpec((rows, d), lambda l, j: (0, 0)),
            pl.BlockSpec((1, d, tn), lambda l, j: (l, 0, j)),
            pl.BlockSpec((1, 1, tn), lambda l, j: (l, 0, j)),
        ],
        out_specs=pl.BlockSpec((1, rows, tn), lambda l, j: (l, 0, j)),
        out_shape=jax.ShapeDtypeStruct((depth, rows, n), F32),
        compiler_params=_params("parallel", "parallel"),
        name="modulation",
    )(c_all, w_mod, b_mod.reshape(depth, 1, n))
    return out.reshape(depth, rows, N_MOD, d)


def _ffn_kernel(x_ref, mod_ref, g_ref, win_ref, wout_ref, *rest, row0, d_ff, final):
    o_ref = rest[-1]
    x = x_ref[0]
    shift = mod_ref[0, row0:row0 + 1, :]
    scale = mod_ref[0, row0 + 1:row0 + 2, :]
    gate = mod_ref[0, row0 + 2:row0 + 3, :]
    h = _ada_norm(x, g_ref[...], shift, scale).astype(BF16)
    y = jnp.dot(h, win_ref[...], preferred_element_type=F32)
    g, up = y[:, :d_ff], y[:, d_ff:]
    a = (g * jax.nn.sigmoid(g) * up).astype(BF16)
    out = jnp.dot(a, wout_ref[...], preferred_element_type=F32)
    x_new = x + (0.5 * gate) * out
    if final:
        x_new = _rms(x_new) * rest[0][...]
    o_ref[0] = x_new


def _ffn(x, mods_l, mod_row, row0, g, w_in, w_out, final_g=None):
    b, n, d = x.shape
    d_ff = w_out.shape[0]
    tm = _tile(n, FFN_ROWS, V7X_SUBLANES)
    final = final_g is not None
    in_specs = [
        pl.BlockSpec((1, tm, d), lambda bi, i: (bi, i, 0)),
        pl.BlockSpec((1, N_MOD, d), lambda bi, i: (mod_row(bi), 0, 0)),
        _resident((1, d)),
        _resident((d, 2 * d_ff)),
        _resident((d_ff, d)),
    ]
    args = [x, mods_l, g.reshape(1, d), w_in, w_out]
    if final:
        in_specs.append(_resident((1, d)))
        args.append(final_g.reshape(1, d))
    return pl.pallas_call(
        functools.partial(_ffn_kernel, row0=row0, d_ff=d_ff, final=final),
        grid=(b, n // tm),
        in_specs=in_specs,
        out_specs=pl.BlockSpec((1, tm, d), lambda bi, i: (bi, i, 0)),
        out_shape=jax.ShapeDtypeStruct((b, n, d), F32),
        compiler_params=_params("parallel", "parallel"),
        name="ffn",
    )(*args)


def _head_norm_rope(z, g, cs, sn):
    p = ROPE_PAIRS
    ms = jnp.sum(z * z, axis=0, keepdims=True) * (1.0 / HEAD_DIM)
    zn = (z * lax.rsqrt(ms + EPS)) * g
    a1, a2, b1, b2 = zn[0:p], zn[p:2 * p], zn[2 * p:3 * p], zn[3 * p:4 * p]
    cr, cc = cs[0:p], cs[p:2 * p]
    sr, sc = sn[0:p], sn[p:2 * p]
    return jnp.concatenate(
        [a1 * cr - a2 * sr, a2 * cr + a1 * sr, b1 * cc - b2 * sc, b2 * cc + b1 * sc], axis=0)


def _inproj_kernel(x_ref, mod_ref, g_ref, wt_ref, wn_ref, qg_ref, kg_ref, cos_ref, sin_ref,
                   qt_ref, k_ref, vt_ref, yn_ref, *, n_heads, n_kv):
    x = x_ref[0]
    shift = mod_ref[0, 3:4, :]
    scale = mod_ref[0, 4:5, :]
    h = _ada_norm(x, g_ref[...], shift, scale).astype(BF16)
    yn_ref[0] = jnp.dot(h, wn_ref[...], preferred_element_type=F32)
    yt = lax.dot_general(wt_ref[...], h, (((1,), (1,)), ((), ())), preferred_element_type=F32)
    cs, sn = cos_ref[...], sin_ref[...]
    tm = x.shape[0]
    kv_rows = n_kv * HEAD_DIM
    for hd in range(n_heads):
        q = _head_norm_rope(yt[hd * HEAD_DIM:(hd + 1) * HEAD_DIM], qg_ref[...], cs, sn)
        q = (q * Q_SCALE).astype(BF16)
        grp, member = divmod(hd, GQA_GROUP)
        cols = slice(member * tm, (member + 1) * tm)
        for other in range(n_kv):
            rows = slice(other * HEAD_DIM, (other + 1) * HEAD_DIM)
            qt_ref[0, grp, rows, cols] = q if other == grp else jnp.zeros_like(q)
    k0 = n_heads * HEAD_DIM
    kt = jnp.concatenate(
        [_head_norm_rope(yt[k0 + j * HEAD_DIM:k0 + (j + 1) * HEAD_DIM], kg_ref[...], cs, sn)
         for j in range(n_kv)], axis=0)
    k_ref[0] = kt.T.astype(BF16)
    v0 = k0 + kv_rows
    ones_row = (lax.broadcasted_iota(jnp.int32, (V7X_BF16_ROWS, tm), 0) == 0).astype(BF16)
    for j in range(n_kv):
        vt_ref[0, j, 0:HEAD_DIM, :] = yt[v0 + j * HEAD_DIM:v0 + (j + 1) * HEAD_DIM].astype(BF16)
        vt_ref[0, j, HEAD_DIM:HEAD_DIM + V7X_BF16_ROWS, :] = ones_row


def _inproj(x, mods_l, mod_row, g, w_t, w_n, q_g, k_g, cos_t, sin_t, n_heads, n_kv):
    b, n, d = x.shape
    n_t, n_n = w_t.shape[0], w_n.shape[1]
    kv_rows = n_kv * HEAD_DIM
    v_rows = HEAD_DIM + V7X_BF16_ROWS
    tm = _tile(n, PROJ_ROWS, V7X_LANES)
    assert n_heads == n_kv * GQA_GROUP
    qg_t = jnp.broadcast_to(q_g[:, None], (HEAD_DIM, tm))
    kg_t = jnp.broadcast_to(k_g[:, None], (HEAD_DIM, tm))
    outs = pl.pallas_call(
        functools.partial(_inproj_kernel, n_heads=n_heads, n_kv=n_kv),
        grid=(b, n // tm),
        in_specs=[
            pl.BlockSpec((1, tm, d), lambda bi, i: (bi, i, 0)),
            pl.BlockSpec((1, N_MOD, d), lambda bi, i: (mod_row(bi), 0, 0)),
            _resident((1, d)),
            _resident((n_t, d)),
            _resident((d, n_n)),
            _resident((HEAD_DIM, tm)),
            _resident((HEAD_DIM, tm)),
            pl.BlockSpec((2 * ROPE_PAIRS, tm), lambda bi, i: (0, i)),
            pl.BlockSpec((2 * ROPE_PAIRS, tm), lambda bi, i: (0, i)),
        ],
        out_specs=[
            pl.BlockSpec((1, n_kv, kv_rows, GQA_GROUP * tm), lambda bi, i: (bi, 0, 0, i)),
            pl.BlockSpec((1, tm, kv_rows), lambda bi, i: (bi, i, 0)),
            pl.BlockSpec((1, n_kv, v_rows, tm), lambda bi, i: (bi, 0, 0, i)),
            pl.BlockSpec((1, tm, n_n), lambda bi, i: (bi, i, 0)),
        ],
        out_shape=[
            jax.ShapeDtypeStruct((b, n_kv, kv_rows, GQA_GROUP * n), BF16),
            jax.ShapeDtypeStruct((b, n, kv_rows), BF16),
            jax.ShapeDtypeStruct((b, n_kv, v_rows, n), BF16),
            jax.ShapeDtypeStruct((b, n, n_n), F32),
        ],
        compiler_params=_params("parallel", "parallel"),
        name="inproj",
    )(x, mods_l, g.reshape(1, d), w_t, w_n, qg_t, kg_t, cos_t, sin_t)
    return (*outs, tm)


def _attn_kernel(qt_ref, k_ref, vt_ref, o_ref, m_ref, acc_ref):
    j = pl.program_id(3)
    tq = o_ref.shape[1]
    qt = qt_ref[0, 0]
    sample = min(M_SAMPLE_KEYS, k_ref.shape[1])

    @pl.when(j == 0)
    def _():
        s0 = jnp.dot(k_ref[0, 0:sample, :], qt, preferred_element_type=F32)
        m_ref[...] = jnp.max(s0, axis=0, keepdims=True)
        acc_ref[...] = jnp.zeros(acc_ref.shape, F32)

    s = jnp.dot(k_ref[0], qt, preferred_element_type=F32)
    p = jnp.exp2(s - m_ref[...]).astype(BF16)
    pv = jnp.dot(vt_ref[0, 0], p, preferred_element_type=F32)
    safe = jnp.max(pv[HEAD_DIM:HEAD_DIM + 1]) <= P_SUM_LIMIT

    @pl.when(safe)
    def _():
        acc_ref[...] += pv

    @pl.when(jnp.logical_not(safe))
    def _():
        s2 = jnp.dot(k_ref[0], qt, preferred_element_type=F32)
        m_prev = m_ref[...]
        m_new = jnp.maximum(m_prev, jnp.max(s2, axis=0, keepdims=True))
        p2 = jnp.exp2(s2 - m_new).astype(BF16)
        acc_ref[...] = (acc_ref[...] * jnp.exp2(m_prev - m_new)
                        + jnp.dot(vt_ref[0, 0], p2, preferred_element_type=F32))
        m_ref[...] = m_new

    @pl.when(j == pl.num_programs(3) - 1)
    def _():
        acc = acc_ref[...]
        o = acc[0:HEAD_DIM] / acc[HEAD_DIM:HEAD_DIM + 1]
        o = jnp.concatenate([o[:, h * tq:(h + 1) * tq] for h in range(GQA_GROUP)], axis=0)
        o_ref[0] = o.T


def _attention(qt, k, vt, tq):
    b, n_kv, kv_rows, lq_g = qt.shape
    lq = lq_g // GQA_GROUP
    lk = k.shape[1]
    v_rows = vt.shape[2]
    tk = _tile(lk, ATTN_K_ROWS, V7X_LANES)
    grp_w = GQA_GROUP * HEAD_DIM
    return pl.pallas_call(
        _attn_kernel,
        grid=(b, n_kv, lq // tq, lk // tk),
        in_specs=[
            pl.BlockSpec((1, 1, kv_rows, GQA_GROUP * tq), lambda bi, g, i, j: (bi, g, 0, i)),
            pl.BlockSpec((1, tk, kv_rows), lambda bi, g, i, j: (bi, j, 0)),
            pl.BlockSpec((1, 1, v_rows, tk), lambda bi, g, i, j: (bi, g, 0, j)),
        ],
        out_specs=pl.BlockSpec((1, tq, grp_w), lambda bi, g, i, j: (bi, i, g)),
        out_shape=jax.ShapeDtypeStruct((b, lq, n_kv * grp_w), F32),
        scratch_shapes=[
            pltpu.VMEM((1, GQA_GROUP * tq), F32),
            pltpu.VMEM((v_rows, GQA_GROUP * tq), F32),
        ],
        compiler_params=_params("parallel", "parallel", "parallel", "arbitrary"),
        name="attention",
    )(qt, k, vt)


def _scan_groups(a, b, reverse):
    n, w = a.shape
    grouped = (n // V7X_SUBLANES, V7X_SUBLANES, w)
    a = a.reshape(grouped)
    b = b.reshape(grouped)
    sub = lax.broadcasted_iota(jnp.int32, grouped, 1)
    d = 1
    while d < V7X_SUBLANES:
        shift = V7X_SUBLANES - d if reverse else d
        valid = sub < V7X_SUBLANES - d if reverse else sub >= d
        a_s = pltpu.roll(a, shift, 1)
        b_s = pltpu.roll(b, shift, 1)
        b = jnp.where(valid, a * b_s + b, b)
        a = jnp.where(valid, a * a_s, a)
        d *= 2
    return a.reshape(n, w), b.reshape(n, w)


def _lru_kernel(u_ref, prev_ref, next_ref, cw_ref, cb_ref, wa_ref, ba_ref, wx_ref, bx_ref, lam_ref,
                h0_ref, h_ref, hl_ref, xe_ref, carry_ref, *, reverse):
    j = pl.program_id(1)
    nc = pl.num_programs(1)
    pos = nc - 1 - j if reverse else j
    tc = u_ref.shape[1]

    @pl.when(j == 0)
    def _():
        carry_ref[...] = h0_ref[0]

    xe_ref[0:HALO, :] = jnp.where(pos > 0, prev_ref[0], 0.0)
    xe_ref[HALO:HALO + tc, :] = u_ref[0]
    xe_ref[HALO + tc:, :] = jnp.where(pos < nc - 1, next_ref[0], 0.0)
    uc = cb_ref[...]
    for t in range(LRU_CONV_W):
        uc = uc + cw_ref[t:t + 1, :] * xe_ref[pl.ds(HALO - LRU_PAD_LEFT + t, tc), :]

    ub = uc.astype(BF16)
    r = jax.nn.sigmoid(jnp.dot(ub, wa_ref[...], preferred_element_type=F32) + ba_ref[...])
    i = jax.nn.sigmoid(jnp.dot(ub, wx_ref[...], preferred_element_type=F32) + bx_ref[...])
    nlam = -lam_ref[...]
    softplus = jnp.maximum(nlam, 0.0) + jnp.log1p(jnp.exp(-jnp.abs(nlam)))
    log_a = (-LRU_C * r) * softplus
    a = jnp.exp(log_a)
    xin = jnp.sqrt(-jnp.tanh(log_a) * (a * a + 1.0)) * i * uc

    grp_a, grp_b = _scan_groups(a, xin, reverse)
    h_in = carry_ref[...]
    n_groups = tc // V7X_SUBLANES
    for step in range(n_groups):
        g = n_groups - 1 - step if reverse else step
        rows = slice(g * V7X_SUBLANES, (g + 1) * V7X_SUBLANES)
        h = grp_a[rows] * h_in + grp_b[rows]
        h_ref[0, rows, :] = h
        h_in = h[0:1] if reverse else h[V7X_SUBLANES - 1:V7X_SUBLANES]
    carry_ref[...] = h_in
    hl_ref[0] = h_in


def _lru(yn, col, conv_w, conv_b, wa, ba, wx, bx, lam, h0, reverse):
    b, n, _ = yn.shape
    w = wa.shape[0]
    tc = _tile(n, LRU_ROWS, V7X_SUBLANES)
    nc = n // tc
    per = tc // HALO
    last_halo = n // HALO - 1

    def pos(j):
        return nc - 1 - j if reverse else j

    vec = lambda v: v.reshape(1, w)
    return pl.pallas_call(
        functools.partial(_lru_kernel, reverse=reverse),
        grid=(b, nc),
        in_specs=[
            pl.BlockSpec((1, tc, w), lambda bi, j: (bi, pos(j), col)),
            pl.BlockSpec((1, HALO, w), lambda bi, j: (bi, jnp.maximum(pos(j) * per - 1, 0), col)),
            pl.BlockSpec((1, HALO, w), lambda bi, j: (bi, jnp.minimum((pos(j) + 1) * per, last_halo), col)),
            _resident((LRU_CONV_W, w)),
            _resident((1, w)),
            _resident((w, w)),
            _resident((1, w)),
            _resident((w, w)),
            _resident((1, w)),
            _resident((1, w)),
            pl.BlockSpec((1, 1, w), lambda bi, j: (bi, 0, 0)),
        ],
        out_specs=[
            pl.BlockSpec((1, tc, w), lambda bi, j: (bi, pos(j), 0)),
            pl.BlockSpec((1, 1, w), lambda bi, j: (bi, 0, 0)),
        ],
        out_shape=[
            jax.ShapeDtypeStruct((b, n, w), F32),
            jax.ShapeDtypeStruct((b, 1, w), F32),
        ],
        scratch_shapes=[
            pltpu.VMEM((tc + 2 * HALO, w), F32),
            pltpu.VMEM((1, w), F32),
        ],
        compiler_params=_params("parallel", "arbitrary"),
        name="lru_bwd" if reverse else "lru_fwd",
    )(yn, yn, yn, conv_w, vec(conv_b), wa, vec(ba), wx, vec(bx), vec(lam), h0)


def _merge_kernel(attn_ref, hf_ref, hb_ref, gx_ref, bg_ref, cg_ref, sx_ref, cgp_ref, sxp_ref,
                  cgn_ref, sxn_ref, x_ref, mod_ref, g_ref, w_ref, cw_ref, cb_ref, o_ref, pe_ref):
    i = pl.program_id(1)
    ni = pl.num_programs(1)
    tm = x_ref.shape[1]

    pe_ref[0:HALO, :] = jnp.where(i > 0, cgp_ref[0] * sxp_ref[0], 0.0)
    pe_ref[HALO:HALO + tm, :] = cg_ref[0] * sx_ref[0]
    pe_ref[HALO + tm:, :] = jnp.where(i < ni - 1, cgn_ref[0] * sxn_ref[0], 0.0)
    conv = cb_ref[...]
    for t in range(SC_CONV_W):
        conv = conv + cw_ref[t:t + 1, :] * pe_ref[pl.ds(HALO - SC_PAD_LEFT + t, tm), :]
    sc = bg_ref[0] * conv

    lru = (hf_ref[0] + hb_ref[0]) * _gelu_tanh(gx_ref[0])
    cat = jnp.concatenate([_rms(attn_ref[0]), _rms(lru), _rms(sc)], axis=-1) * g_ref[...]
    out = jnp.dot(cat.astype(BF16), w_ref[...], preferred_element_type=F32)
    o_ref[0] = x_ref[0] + mod_ref[0, 5:6, :] * out


def _merge(attn, hf, hb, yn, x, mods_l, mod_row, grp_g, w_out, sc_w, sc_b):
    b, n, d = x.shape
    w = hf.shape[2]
    aw = attn.shape[2]
    tm = _tile(n, MERGE_ROWS, V7X_SUBLANES)
    per = tm // HALO
    last_halo = n // HALO - 1
    main = lambda col: pl.BlockSpec((1, tm, w), lambda bi, i: (bi, i, col))
    prev = lambda col: pl.BlockSpec((1, HALO, w), lambda bi, i: (bi, jnp.maximum(i * per - 1, 0), col))
    nxt = lambda col: pl.BlockSpec((1, HALO, w), lambda bi, i: (bi, jnp.minimum((i + 1) * per, last_halo), col))
    return pl.pallas_call(
        _merge_kernel,
        grid=(b, n // tm),
        in_specs=[
            pl.BlockSpec((1, tm, aw), lambda bi, i: (bi, i, 0)),
            main(0), main(0),
            main(1), main(2), main(3), main(4),
            prev(3), prev(4), nxt(3), nxt(4),
            pl.BlockSpec((1, tm, d), lambda bi, i: (bi, i, 0)),
            pl.BlockSpec((1, N_MOD, d), lambda bi, i: (mod_row(bi), 0, 0)),
            _resident((1, d)),
            _resident((d, d)),
            _resident((SC_CONV_W, w)),
            _resident((1, w)),
        ],
        out_specs=pl.BlockSpec((1, tm, d), lambda bi, i: (bi, i, 0)),
        out_shape=jax.ShapeDtypeStruct((b, n, d), F32),
        scratch_shapes=[pltpu.VMEM((tm + 2 * HALO, w), F32)],
        compiler_params=_params("parallel", "arbitrary"),
        name="merge",
    )(attn, hf, hb, yn, yn, yn, yn, yn, yn, yn, yn, x, mods_l, grp_g.reshape(1, d), w_out, sc_w,
      sc_b.reshape(1, w))


def _rope_tables_t(seq):
    rows = seq // GRID_W
    row_ids = jnp.repeat(jnp.arange(rows), GRID_W).astype(F32)
    col_ids = jnp.tile(jnp.arange(GRID_W), rows).astype(F32)
    inv_freq = ROPE_THETA ** (-jnp.arange(ROPE_PAIRS, dtype=F32) / ROPE_PAIRS)
    ang = jnp.stack([row_ids[:, None] * inv_freq, col_ids[:, None] * inv_freq], axis=1)
    ang_t = ang.reshape(seq, 2 * ROPE_PAIRS).T
    return jnp.cos(ang_t), jnp.sin(ang_t)


def _block_diag(w):
    nb, n, _ = w.shape
    eye = jnp.eye(nb, dtype=w.dtype)
    return (eye[:, None, :, None] * w[:, :, None, :]).reshape(nb * n, nb * n)


def kernel(x, c, ctx, c_ctx, w_mod, b_mod, norm_g, w_ffn_in, w_ffn_out, w_in, q_norm_g, k_norm_g,
           lru_conv_w, lru_conv_b, lru_wa, lru_ba, lru_wx, lru_bx, lru_lambda, sc_conv_w, sc_conv_b,
           grp_norm_g, w_out, final_norm_g):
    bsz, seq, d = x.shape
    n_ctx = ctx.shape[1]
    depth = w_mod.shape[0]
    attn_w = d // 2
    n_heads = attn_w // HEAD_DIM
    n_kv = n_heads // GQA_GROUP
    kv_w = n_kv * HEAD_DIM
    lru_w = lru_conv_w.shape[2]
    n_t = attn_w + 2 * kv_w

    rows = -(-(bsz + 1) // V7X_SUBLANES) * V7X_SUBLANES
    c_all = jnp.zeros((rows, d), F32).at[:bsz].set(c).at[bsz].set(c_ctx)
    mods = _modulation(c_all, w_mod, b_mod)
    x_row = lambda bi: bi
    ctx_row = lambda bi: bsz

    cos_x, sin_x = _rope_tables_t(seq)
    cos_c = jnp.ones((2 * ROPE_PAIRS, n_ctx), F32)
    sin_c = jnp.zeros((2 * ROPE_PAIRS, n_ctx), F32)
    zero_state = jnp.zeros((bsz, 1, lru_w), F32)

    h_ctx = ctx
    for l in range(depth):
        last = l == depth - 1
        wf_in = w_ffn_in[l].astype(BF16)
        wf_out = w_ffn_out[l].astype(BF16)
        w_t = w_in[l][:, :n_t].T.astype(BF16)
        w_n = w_in[l][:, n_t:].astype(BF16)
        wo = w_out[l].astype(BF16)
        lru_p = [(lru_conv_w[l], lru_conv_b[l], _block_diag(lru_wa[l, dr]).astype(BF16), lru_ba[l, dr],
                  _block_diag(lru_wx[l, dr]).astype(BF16), lru_bx[l, dr], lru_lambda[l, dr]) for dr in range(2)]

        x = _ffn(x, mods[l], x_row, 0, norm_g[l, 0], wf_in[0], wf_out[0])
        h_ctx = _ffn(h_ctx, mods[l], ctx_row, 0, norm_g[l, 0], wf_in[0], wf_out[0])

        proj = functools.partial(_inproj, g=norm_g[l, 1], w_t=w_t, w_n=w_n, q_g=q_norm_g[l], k_g=k_norm_g[l],
                                 n_heads=n_heads, n_kv=n_kv)
        qt_x, k_x, vt_x, yn_x, tq_x = proj(x, mods[l], x_row, cos_t=cos_x, sin_t=sin_x)
        qt_c, k_c, vt_c, yn_c, tq_c = proj(h_ctx, mods[l], ctx_row, cos_t=cos_c, sin_t=sin_c)

        attn_x = _attention(qt_x, jnp.concatenate([k_c, k_x], axis=1), jnp.concatenate([vt_c, vt_x], axis=3),
                            tq_x)

        hc_f, state_f = _lru(yn_c, 0, *lru_p[0], zero_state, reverse=False)
        hc_b, state_b = _lru(yn_c, 0, *lru_p[1], zero_state, reverse=True)
        hx_f, _ = _lru(yn_x, 0, *lru_p[0], state_f, reverse=False)
        hx_b, _ = _lru(yn_x, 0, *lru_p[1], state_b, reverse=True)

        x = _merge(attn_x, hx_f, hx_b, yn_x, x, mods[l], x_row, grp_norm_g[l], wo, sc_conv_w[l], sc_conv_b[l])
        x = _ffn(x, mods[l], x_row, 6, norm_g[l, 2], wf_in[1], wf_out[1],
                 final_g=final_norm_g if last else None)
        if not last:
            attn_c = _attention(qt_c, k_c, vt_c, tq_c)
            h_ctx = _merge(attn_c, hc_f, hc_b, yn_c, h_ctx, mods[l], ctx_row, grp_norm_g[l], wo,
                           sc_conv_w[l], sc_conv_b[l])
            h_ctx = _ffn(h_ctx, mods[l], ctx_row, 6, norm_g[l, 2], wf_in[1], wf_out[1])
    return x
```

```python
import functools
import math

import jax
import jax.numpy as jnp
from jax import lax
from jax.experimental import pallas as pl
from jax.experimental.pallas import tpu as pltpu

F32 = jnp.float32
BF16 = jnp.bfloat16

HEAD_DIM = 64
GQA_GROUP = 4
ROPE_PAIRS = HEAD_DIM // 4
GRID_W = 64
ROPE_THETA = 10000.0
N_MOD = 9
EPS = 1e-6
LRU_C = 8.0
LRU_CONV_W = 4
LRU_PAD_LEFT = 2
SC_CONV_W = 3
SC_PAD_LEFT = 1
Q_SCALE = HEAD_DIM ** -0.5 * math.log2(math.e)
P_SUM_LIMIT = 2.0 ** 64
M_SAMPLE_KEYS = 128

V7X_SUBLANES = 8
V7X_LANES = 128
V7X_BF16_ROWS = 16
V7X_VMEM_BYTES = 64 * 2 ** 20
VMEM_LIMIT_BYTES = V7X_VMEM_BYTES * 7 // 8

FFN_ROWS = 512
PROJ_ROWS = 512
ATTN_K_ROWS = 2816
LRU_ROWS = 512
MERGE_ROWS = 256
HALO = V7X_SUBLANES


def _tile(length, target, align):
    best = None
    for t in range(align, min(length, target) + 1, align):
        if length % t == 0:
            best = t
    if best is None:
        raise ValueError(f"no {align}-aligned tile divides {length}")
    return best


def _params(*semantics):
    return pltpu.CompilerParams(dimension_semantics=semantics, vmem_limit_bytes=VMEM_LIMIT_BYTES)


def _resident(shape):
    zeros = (0,) * len(shape)
    return pl.BlockSpec(shape, lambda *_: zeros, pipeline_mode=pl.Buffered(1))


def _rms(t):
    return t * lax.rsqrt(jnp.mean(t * t, axis=-1, keepdims=True) + EPS)


def _ada_norm(x, g, shift, scale):
    return (_rms(x) * g) * (1.0 + scale) + shift


def _gelu_tanh(x):
    cdf = 0.5 * (1.0 + jnp.tanh(math.sqrt(2.0 / math.pi) * (x + 0.044715 * (x * x * x))))
    return x * cdf


def _mod_kernel(c_ref, w_ref, b_ref, o_ref):
    c = c_ref[...]
    a = (c * jax.nn.sigmoid(c)).astype(BF16)
    o_ref[0] = jnp.dot(a, w_ref[0].astype(BF16), preferred_element_type=F32) + b_ref[0]


def _modulation(c_all, w_mod, b_mod):
    depth, d, n = w_mod.shape
    rows = c_all.shape[0]
    tn = _tile(n, d, V7X_LANES)
    out = pl.pallas_call(
        _mod_kernel,
        grid=(depth, n // tn),
        in_specs=[
            pl.BlockSpec((rows, d), lambda l, j: (0, 0)),
            pl.BlockSpec((1, d, tn), lambda l, j: (l, 0, j)),
            pl.BlockSpec((1, 1, tn), lambda l, j: (l, 0, j)),
        ],
        out_specs=pl.BlockSpec((1, rows, tn), lambda l, j: (l, 0, j)),
        out_shape=jax.ShapeDtypeStruct((depth, rows, n), F32),
        compiler_params=_params("parallel", "parallel"),
        name="modulation",
    )(c_all, w_mod, b_mod.reshape(depth, 1, n))
    return out.reshape(depth, rows, N_MOD, d)


def _ffn_block(x, mod_ref, row0, g_ref, win_ref, wout_ref):
    d_ff = wout_ref.shape[0]
    shift = mod_ref[0, row0:row0 + 1, :]
    scale = mod_ref[0, row0 + 1:row0 + 2, :]
    gate = mod_ref[0, row0 + 2:row0 + 3, :]
    h = _ada_norm(x, g_ref[...], shift, scale).astype(BF16)
    y = jnp.dot(h, win_ref[...], preferred_element_type=F32)
    g, up = y[:, :d_ff], y[:, d_ff:]
    a = (g * jax.nn.sigmoid(g) * up).astype(BF16)
    out = jnp.dot(a, wout_ref[...], preferred_element_type=F32)
    return x + (0.5 * gate) * out


def _ffn_kernel(x_ref, mod_ref, g_ref, win_ref, wout_ref, *rest, row0, final):
    o_ref = rest[-1]
    x_new = _ffn_block(x_ref[0], mod_ref, row0, g_ref, win_ref, wout_ref)
    if final:
        x_new = _rms(x_new) * rest[0][...]
    o_ref[0] = x_new


def _ffn(x, mods_l, mod_row, row0, g, w_in, w_out, final_g=None):
    b, n, d = x.shape
    d_ff = w_out.shape[0]
    tm = _tile(n, FFN_ROWS, V7X_SUBLANES)
    final = final_g is not None
    in_specs = [
        pl.BlockSpec((1, tm, d), lambda bi, i: (bi, i, 0)),
        pl.BlockSpec((1, N_MOD, d), lambda bi, i: (mod_row(bi), 0, 0)),
        _resident((1, d)),
        _resident((d, 2 * d_ff)),
        _resident((d_ff, d)),
    ]
    args = [x, mods_l, g.reshape(1, d), w_in, w_out]
    if final:
        in_specs.append(_resident((1, d)))
        args.append(final_g.reshape(1, d))
    return pl.pallas_call(
        functools.partial(_ffn_kernel, row0=row0, final=final),
        grid=(b, n // tm),
        in_specs=in_specs,
        out_specs=pl.BlockSpec((1, tm, d), lambda bi, i: (bi, i, 0)),
        out_shape=jax.ShapeDtypeStruct((b, n, d), F32),
        compiler_params=_params("parallel", "parallel"),
        name="ffn",
    )(*args)


def _head_norm_rope(z, g, cs, sn):
    p = ROPE_PAIRS
    ms = jnp.sum(z * z, axis=0, keepdims=True) * (1.0 / HEAD_DIM)
    zn = (z * lax.rsqrt(ms + EPS)) * g
    a1, a2, b1, b2 = zn[0:p], zn[p:2 * p], zn[2 * p:3 * p], zn[3 * p:4 * p]
    cr, cc = cs[0:p], cs[p:2 * p]
    sr, sc = sn[0:p], sn[p:2 * p]
    return jnp.concatenate(
        [a1 * cr - a2 * sr, a2 * cr + a1 * sr, b1 * cc - b2 * sc, b2 * cc + b1 * sc], axis=0)


def _inproj_kernel(x_ref, mod_ref, g_ref, wt_ref, wn_ref, qg_ref, kg_ref, cos_ref, sin_ref,
                   qt_ref, k_ref, vt_ref, yn_ref, *, n_heads, n_kv):
    x = x_ref[0]
    shift = mod_ref[0, 3:4, :]
    scale = mod_ref[0, 4:5, :]
    h = _ada_norm(x, g_ref[...], shift, scale).astype(BF16)
    yn_ref[0] = jnp.dot(h, wn_ref[...], preferred_element_type=F32)
    yt = lax.dot_general(wt_ref[...], h, (((1,), (1,)), ((), ())), preferred_element_type=F32)
    cs, sn = cos_ref[...], sin_ref[...]
    tm = x.shape[0]
    kv_rows = n_kv * HEAD_DIM
    for hd in range(n_heads):
        q = _head_norm_rope(yt[hd * HEAD_DIM:(hd + 1) * HEAD_DIM], qg_ref[...], cs, sn)
        q = (q * Q_SCALE).astype(BF16)
        grp, member = divmod(hd, GQA_GROUP)
        cols = slice(member * tm, (member + 1) * tm)
        for other in range(n_kv):
            rows = slice(other * HEAD_DIM, (other + 1) * HEAD_DIM)
            qt_ref[0, grp, rows, cols] = q if other == grp else jnp.zeros_like(q)
    k0 = n_heads * HEAD_DIM
    kt = jnp.concatenate(
        [_head_norm_rope(yt[k0 + j * HEAD_DIM:k0 + (j + 1) * HEAD_DIM], kg_ref[...], cs, sn)
         for j in range(n_kv)], axis=0)
    k_ref[0] = kt.T.astype(BF16)
    v0 = k0 + kv_rows
    ones_row = (lax.broadcasted_iota(jnp.int32, (V7X_BF16_ROWS, tm), 0) == 0).astype(BF16)
    for j in range(n_kv):
        vt_ref[0, j, 0:HEAD_DIM, :] = yt[v0 + j * HEAD_DIM:v0 + (j + 1) * HEAD_DIM].astype(BF16)
        vt_ref[0, j, HEAD_DIM:HEAD_DIM + V7X_BF16_ROWS, :] = ones_row


def _inproj(x, mods_l, mod_row, g, w_t, w_n, q_g, k_g, cos_t, sin_t, n_heads, n_kv):
    b, n, d = x.shape
    n_t, n_n = w_t.shape[0], w_n.shape[1]
    kv_rows = n_kv * HEAD_DIM
    v_rows = HEAD_DIM + V7X_BF16_ROWS
    tm = _tile(n, PROJ_ROWS, V7X_LANES)
    assert n_heads == n_kv * GQA_GROUP
    qg_t = jnp.broadcast_to(q_g[:, None], (HEAD_DIM, tm))
    kg_t = jnp.broadcast_to(k_g[:, None], (HEAD_DIM, tm))
    outs = pl.pallas_call(
        functools.partial(_inproj_kernel, n_heads=n_heads, n_kv=n_kv),
        grid=(b, n // tm),
        in_specs=[
            pl.BlockSpec((1, tm, d), lambda bi, i: (bi, i, 0)),
            pl.BlockSpec((1, N_MOD, d), lambda bi, i: (mod_row(bi), 0, 0)),
            _resident((1, d)),
            _resident((n_t, d)),
            _resident((d, n_n)),
            _resident((HEAD_DIM, tm)),
            _resident((HEAD_DIM, tm)),
            pl.BlockSpec((2 * ROPE_PAIRS, tm), lambda bi, i: (0, i)),
            pl.BlockSpec((2 * ROPE_PAIRS, tm), lambda bi, i: (0, i)),
        ],
        out_specs=[
            pl.BlockSpec((1, n_kv, kv_rows, GQA_GROUP * tm), lambda bi, i: (bi, 0, 0, i)),
            pl.BlockSpec((1, tm, kv_rows), lambda bi, i: (bi, i, 0)),
            pl.BlockSpec((1, n_kv, v_rows, tm), lambda bi, i: (bi, 0, 0, i)),
            pl.BlockSpec((1, tm, n_n), lambda bi, i: (bi, i, 0)),
        ],
        out_shape=[
            jax.ShapeDtypeStruct((b, n_kv, kv_rows, GQA_GROUP * n), BF16),
            jax.ShapeDtypeStruct((b, n, kv_rows), BF16),
            jax.ShapeDtypeStruct((b, n_kv, v_rows, n), BF16),
            jax.ShapeDtypeStruct((b, n, n_n), F32),
        ],
        compiler_params=_params("parallel", "parallel"),
        name="inproj",
    )(x, mods_l, g.reshape(1, d), w_t, w_n, qg_t, kg_t, cos_t, sin_t)
    return (*outs, tm)


def _attn_kernel(qt_ref, k_ref, vt_ref, o_ref, m_ref, acc_ref):
    j = pl.program_id(3)
    tq = o_ref.shape[1]
    qt = qt_ref[0, 0]
    sample = min(M_SAMPLE_KEYS, k_ref.shape[1])

    @pl.when(j == 0)
    def _():
        s0 = jnp.dot(k_ref[0, 0:sample, :], qt, preferred_element_type=F32)
        m_ref[...] = jnp.max(s0, axis=0, keepdims=True)
        acc_ref[...] = jnp.zeros(acc_ref.shape, F32)

    s = jnp.dot(k_ref[0], qt, preferred_element_type=F32)
    p = jnp.exp2(s - m_ref[...]).astype(BF16)
    pv = jnp.dot(vt_ref[0, 0], p, preferred_element_type=F32)
    safe = jnp.max(pv[HEAD_DIM:HEAD_DIM + 1]) <= P_SUM_LIMIT

    @pl.when(safe)
    def _():
        acc_ref[...] += pv

    @pl.when(jnp.logical_not(safe))
    def _():
        s2 = jnp.dot(k_ref[0], qt, preferred_element_type=F32)
        m_prev = m_ref[...]
        m_new = jnp.maximum(m_prev, jnp.max(s2, axis=0, keepdims=True))
        p2 = jnp.exp2(s2 - m_new).astype(BF16)
        acc_ref[...] = (acc_ref[...] * jnp.exp2(m_prev - m_new)
                        + jnp.dot(vt_ref[0, 0], p2, preferred_element_type=F32))
        m_ref[...] = m_new

    @pl.when(j == pl.num_programs(3) - 1)
    def _():
        acc = acc_ref[...]
        o = acc[0:HEAD_DIM] / acc[HEAD_DIM:HEAD_DIM + 1]
        o = jnp.concatenate([o[:, h * tq:(h + 1) * tq] for h in range(GQA_GROUP)], axis=0)
        o_ref[0] = o.T


def _attention(qt, k, vt, tq):
    b, n_kv, kv_rows, lq_g = qt.shape
    lq = lq_g // GQA_GROUP
    lk = k.shape[1]
    v_rows = vt.shape[2]
    tk = _tile(lk, ATTN_K_ROWS, V7X_LANES)
    grp_w = GQA_GROUP * HEAD_DIM
    return pl.pallas_call(
        _attn_kernel,
        grid=(b, n_kv, lq // tq, lk // tk),
        in_specs=[
            pl.BlockSpec((1, 1, kv_rows, GQA_GROUP * tq), lambda bi, g, i, j: (bi, g, 0, i)),
            pl.BlockSpec((1, tk, kv_rows), lambda bi, g, i, j: (bi, j, 0)),
            pl.BlockSpec((1, 1, v_rows, tk), lambda bi, g, i, j: (bi, g, 0, j)),
        ],
        out_specs=pl.BlockSpec((1, tq, grp_w), lambda bi, g, i, j: (bi, i, g)),
        out_shape=jax.ShapeDtypeStruct((b, lq, n_kv * grp_w), F32),
        scratch_shapes=[
            pltpu.VMEM((1, GQA_GROUP * tq), F32),
            pltpu.VMEM((v_rows, GQA_GROUP * tq), F32),
        ],
        compiler_params=_params("parallel", "parallel", "parallel", "arbitrary"),
        name="attention",
    )(qt, k, vt)


def _scan_groups(a, b, reverse):
    n, w = a.shape
    grouped = (n // V7X_SUBLANES, V7X_SUBLANES, w)
    a = a.reshape(grouped)
    b = b.reshape(grouped)
    sub = lax.broadcasted_iota(jnp.int32, grouped, 1)
    d = 1
    while d < V7X_SUBLANES:
        shift = V7X_SUBLANES - d if reverse else d
        valid = sub < V7X_SUBLANES - d if reverse else sub >= d
        a_s = pltpu.roll(a, shift, 1)
        b_s = pltpu.roll(b, shift, 1)
        b = jnp.where(valid, a * b_s + b, b)
        a = jnp.where(valid, a * a_s, a)
        d *= 2
    return a.reshape(n, w), b.reshape(n, w)


def _lru_kernel(u_ref, prev_ref, next_ref, cw_ref, cb_ref, wa_ref, ba_ref, wx_ref, bx_ref, lam_ref,
                h0_ref, h_ref, hl_ref, xe_ref, carry_ref, *, reverse):
    j = pl.program_id(1)
    nc = pl.num_programs(1)
    pos = nc - 1 - j if reverse else j
    tc = u_ref.shape[1]

    @pl.when(j == 0)
    def _():
        carry_ref[...] = h0_ref[0]

    xe_ref[0:HALO, :] = jnp.where(pos > 0, prev_ref[0], 0.0)
    xe_ref[HALO:HALO + tc, :] = u_ref[0]
    xe_ref[HALO + tc:, :] = jnp.where(pos < nc - 1, next_ref[0], 0.0)
    uc = cb_ref[...]
    for t in range(LRU_CONV_W):
        uc = uc + cw_ref[t:t + 1, :] * xe_ref[pl.ds(HALO - LRU_PAD_LEFT + t, tc), :]

    ub = uc.astype(BF16)
    r = jax.nn.sigmoid(jnp.dot(ub, wa_ref[...], preferred_element_type=F32) + ba_ref[...])
    i = jax.nn.sigmoid(jnp.dot(ub, wx_ref[...], preferred_element_type=F32) + bx_ref[...])
    nlam = -lam_ref[...]
    softplus = jnp.maximum(nlam, 0.0) + jnp.log1p(jnp.exp(-jnp.abs(nlam)))
    log_a = (-LRU_C * r) * softplus
    a = jnp.exp(log_a)
    xin = jnp.sqrt(-jnp.tanh(log_a) * (a * a + 1.0)) * i * uc

    grp_a, grp_b = _scan_groups(a, xin, reverse)
    h_in = carry_ref[...]
    n_groups = tc // V7X_SUBLANES
    for step in range(n_groups):
        g = n_groups - 1 - step if reverse else step
        rows = slice(g * V7X_SUBLANES, (g + 1) * V7X_SUBLANES)
        h = grp_a[rows] * h_in + grp_b[rows]
        h_ref[0, rows, :] = h
        h_in = h[0:1] if reverse else h[V7X_SUBLANES - 1:V7X_SUBLANES]
    carry_ref[...] = h_in
    hl_ref[0] = h_in


def _lru(yn, col, conv_w, conv_b, wa, ba, wx, bx, lam, h0, reverse):
    b, n, _ = yn.shape
    w = wa.shape[0]
    tc = _tile(n, LRU_ROWS, V7X_SUBLANES)
    nc = n // tc
    per = tc // HALO
    last_halo = n // HALO - 1

    def pos(j):
        return nc - 1 - j if reverse else j

    vec = lambda v: v.reshape(1, w)
    return pl.pallas_call(
        functools.partial(_lru_kernel, reverse=reverse),
        grid=(b, nc),
        in_specs=[
            pl.BlockSpec((1, tc, w), lambda bi, j: (bi, pos(j), col)),
            pl.BlockSpec((1, HALO, w), lambda bi, j: (bi, jnp.maximum(pos(j) * per - 1, 0), col)),
            pl.BlockSpec((1, HALO, w), lambda bi, j: (bi, jnp.minimum((pos(j) + 1) * per, last_halo), col)),
            _resident((LRU_CONV_W, w)),
            _resident((1, w)),
            _resident((w, w)),
            _resident((1, w)),
            _resident((w, w)),
            _resident((1, w)),
            _resident((1, w)),
            pl.BlockSpec((1, 1, w), lambda bi, j: (bi, 0, 0)),
        ],
        out_specs=[
            pl.BlockSpec((1, tc, w), lambda bi, j: (bi, pos(j), 0)),
            pl.BlockSpec((1, 1, w), lambda bi, j: (bi, 0, 0)),
        ],
        out_shape=[
            jax.ShapeDtypeStruct((b, n, w), F32),
            jax.ShapeDtypeStruct((b, 1, w), F32),
        ],
        scratch_shapes=[
            pltpu.VMEM((tc + 2 * HALO, w), F32),
            pltpu.VMEM((1, w), F32),
        ],
        compiler_params=_params("parallel", "arbitrary"),
        name="lru_bwd" if reverse else "lru_fwd",
    )(yn, yn, yn, conv_w, vec(conv_b), wa, vec(ba), wx, vec(bx), vec(lam), h0)


def _merge_kernel(attn_ref, hf_ref, hb_ref, gx_ref, bg_ref, cg_ref, sx_ref, cgp_ref, sxp_ref,
                  cgn_ref, sxn_ref, x_ref, mod_ref, g_ref, w_ref, cw_ref, cb_ref,
                  fg_ref, win_ref, wout_ref, *rest, final):
    o_ref, pe_ref = rest[-2:]
    i = pl.program_id(1)
    ni = pl.num_programs(1)
    tm = x_ref.shape[1]

    pe_ref[0:HALO, :] = jnp.where(i > 0, cgp_ref[0] * sxp_ref[0], 0.0)
    pe_ref[HALO:HALO + tm, :] = cg_ref[0] * sx_ref[0]
    pe_ref[HALO + tm:, :] = jnp.where(i < ni - 1, cgn_ref[0] * sxn_ref[0], 0.0)
    conv = cb_ref[...]
    for t in range(SC_CONV_W):
        conv = conv + cw_ref[t:t + 1, :] * pe_ref[pl.ds(HALO - SC_PAD_LEFT + t, tm), :]
    sc = bg_ref[0] * conv

    lru = (hf_ref[0] + hb_ref[0]) * _gelu_tanh(gx_ref[0])
    cat = jnp.concatenate([_rms(attn_ref[0]), _rms(lru), _rms(sc)], axis=-1) * g_ref[...]
    out = jnp.dot(cat.astype(BF16), w_ref[...], preferred_element_type=F32)
    x_mix = x_ref[0] + mod_ref[0, 5:6, :] * out
    x_new = _ffn_block(x_mix, mod_ref, 6, fg_ref, win_ref, wout_ref)
    if final:
        x_new = _rms(x_new) * rest[0][...]
    o_ref[0] = x_new


def _merge_ffn(attn, hf, hb, yn, x, mods_l, mod_row, grp_g, w_out, sc_w, sc_b, ffn_g, w_ffn_in, w_ffn_out,
               final_g=None):
    b, n, d = x.shape
    w = hf.shape[2]
    aw = attn.shape[2]
    d_ff = w_ffn_out.shape[0]
    final = final_g is not None
    tm = _tile(n, MERGE_ROWS, V7X_SUBLANES)
    per = tm // HALO
    last_halo = n // HALO - 1
    main = lambda col: pl.BlockSpec((1, tm, w), lambda bi, i: (bi, i, col))
    prev = lambda col: pl.BlockSpec((1, HALO, w), lambda bi, i: (bi, jnp.maximum(i * per - 1, 0), col))
    nxt = lambda col: pl.BlockSpec((1, HALO, w), lambda bi, i: (bi, jnp.minimum((i + 1) * per, last_halo), col))
    in_specs = [
        pl.BlockSpec((1, tm, aw), lambda bi, i: (bi, i, 0)),
        main(0), main(0),
        main(1), main(2), main(3), main(4),
        prev(3), prev(4), nxt(3), nxt(4),
        pl.BlockSpec((1, tm, d), lambda bi, i: (bi, i, 0)),
        pl.BlockSpec((1, N_MOD, d), lambda bi, i: (mod_row(bi), 0, 0)),
        _resident((1, d)),
        _resident((d, d)),
        _resident((SC_CONV_W, w)),
        _resident((1, w)),
        _resident((1, d)),
        _resident((d, 2 * d_ff)),
        _resident((d_ff, d)),
    ]
    args = [attn, hf, hb, yn, yn, yn, yn, yn, yn, yn, yn, x, mods_l, grp_g.reshape(1, d), w_out, sc_w,
            sc_b.reshape(1, w), ffn_g.reshape(1, d), w_ffn_in, w_ffn_out]
    if final:
        in_specs.append(_resident((1, d)))
        args.append(final_g.reshape(1, d))
    return pl.pallas_call(
        functools.partial(_merge_kernel, final=final),
        grid=(b, n // tm),
        in_specs=in_specs,
        out_specs=pl.BlockSpec((1, tm, d), lambda bi, i: (bi, i, 0)),
        out_shape=jax.ShapeDtypeStruct((b, n, d), F32),
        scratch_shapes=[pltpu.VMEM((tm + 2 * HALO, w), F32)],
        compiler_params=_params("parallel", "parallel"),
        name="merge_ffn",
    )(*args)


def _rope_tables_t(seq):
    rows = seq // GRID_W
    row_ids = jnp.repeat(jnp.arange(rows), GRID_W).astype(F32)
    col_ids = jnp.tile(jnp.arange(GRID_W), rows).astype(F32)
    inv_freq = ROPE_THETA ** (-jnp.arange(ROPE_PAIRS, dtype=F32) / ROPE_PAIRS)
    ang = jnp.stack([row_ids[:, None] * inv_freq, col_ids[:, None] * inv_freq], axis=1)
    ang_t = ang.reshape(seq, 2 * ROPE_PAIRS).T
    return jnp.cos(ang_t), jnp.sin(ang_t)


def _block_diag(w):
    nb, n, _ = w.shape
    eye = jnp.eye(nb, dtype=w.dtype)
    return (eye[:, None, :, None] * w[:, :, None, :]).reshape(nb * n, nb * n)


def kernel(x, c, ctx, c_ctx, w_mod, b_mod, norm_g, w_ffn_in, w_ffn_out, w_in, q_norm_g, k_norm_g,
           lru_conv_w, lru_conv_b, lru_wa, lru_ba, lru_wx, lru_bx, lru_lambda, sc_conv_w, sc_conv_b,
           grp_norm_g, w_out, final_norm_g):
    bsz, seq, d = x.shape
    n_ctx = ctx.shape[1]
    depth = w_mod.shape[0]
    attn_w = d // 2
    n_heads = attn_w // HEAD_DIM
    n_kv = n_heads // GQA_GROUP
    kv_w = n_kv * HEAD_DIM
    lru_w = lru_conv_w.shape[2]
    n_t = attn_w + 2 * kv_w

    rows = -(-(bsz + 1) // V7X_SUBLANES) * V7X_SUBLANES
    c_all = jnp.zeros((rows, d), F32).at[:bsz].set(c).at[bsz].set(c_ctx)
    mods = _modulation(c_all, w_mod, b_mod)
    x_row = lambda bi: bi
    ctx_row = lambda bi: bsz

    cos_x, sin_x = _rope_tables_t(seq)
    cos_c = jnp.ones((2 * ROPE_PAIRS, n_ctx), F32)
    sin_c = jnp.zeros((2 * ROPE_PAIRS, n_ctx), F32)
    zero_state = jnp.zeros((bsz, 1, lru_w), F32)

    h_ctx = ctx
    for l in range(depth):
        last = l == depth - 1
        wf_in = w_ffn_in[l].astype(BF16)
        wf_out = w_ffn_out[l].astype(BF16)
        w_t = w_in[l][:, :n_t].T.astype(BF16)
        w_n = w_in[l][:, n_t:].astype(BF16)
        wo = w_out[l].astype(BF16)
        lru_p = [(lru_conv_w[l], lru_conv_b[l], _block_diag(lru_wa[l, dr]).astype(BF16), lru_ba[l, dr],
                  _block_diag(lru_wx[l, dr]).astype(BF16), lru_bx[l, dr], lru_lambda[l, dr]) for dr in range(2)]

        x = _ffn(x, mods[l], x_row, 0, norm_g[l, 0], wf_in[0], wf_out[0])
        h_ctx = _ffn(h_ctx, mods[l], ctx_row, 0, norm_g[l, 0], wf_in[0], wf_out[0])

        proj = functools.partial(_inproj, g=norm_g[l, 1], w_t=w_t, w_n=w_n, q_g=q_norm_g[l], k_g=k_norm_g[l],
                                 n_heads=n_heads, n_kv=n_kv)
        qt_x, k_x, vt_x, yn_x, tq_x = proj(x, mods[l], x_row, cos_t=cos_x, sin_t=sin_x)
        qt_c, k_c, vt_c, yn_c, tq_c = proj(h_ctx, mods[l], ctx_row, cos_t=cos_c, sin_t=sin_c)

        attn_x = _attention(qt_x, jnp.concatenate([k_c, k_x], axis=1), jnp.concatenate([vt_c, vt_x], axis=3),
                            tq_x)

        hc_f, state_f = _lru(yn_c, 0, *lru_p[0], zero_state, reverse=False)
        hc_b, state_b = _lru(yn_c, 0, *lru_p[1], zero_state, reverse=True)
        hx_f, _ = _lru(yn_x, 0, *lru_p[0], state_f, reverse=False)
        hx_b, _ = _lru(yn_x, 0, *lru_p[1], state_b, reverse=True)

        x = _merge_ffn(attn_x, hx_f, hx_b, yn_x, x, mods[l], x_row, grp_norm_g[l], wo, sc_conv_w[l],
                       sc_conv_b[l], norm_g[l, 2], wf_in[1], wf_out[1], final_g=final_norm_g if last else None)
        if not last:
            attn_c = _attention(qt_c, k_c, vt_c, tq_c)
            h_ctx = _merge_ffn(attn_c, hc_f, hc_b, yn_c, h_ctx, mods[l], ctx_row, grp_norm_g[l], wo,
                               sc_conv_w[l], sc_conv_b[l], norm_g[l, 2], wf_in[1], wf_out[1])
    return x
```

```python
import functools
import math

import jax
import jax.numpy as jnp
from jax import lax
from jax.experimental import pallas as pl
from jax.experimental.pallas import tpu as pltpu

F32 = jnp.float32
BF16 = jnp.bfloat16

HEAD_DIM = 64
GQA_GROUP = 4
ROPE_PAIRS = HEAD_DIM // 4
GRID_W = 64
ROPE_THETA = 10000.0
N_MOD = 9
EPS = 1e-6
LRU_C = 8.0
LRU_CONV_W = 4
LRU_PAD_LEFT = 2
SC_CONV_W = 3
SC_PAD_LEFT = 1
Q_SCALE = HEAD_DIM ** -0.5 * math.log2(math.e)
P_SUM_LIMIT = 2.0 ** 64
M_SAMPLE_KEYS = 128

V7X_SUBLANES = 8
V7X_LANES = 128
V7X_BF16_ROWS = 16
V7X_VMEM_BYTES = 64 * 2 ** 20
VMEM_LIMIT_BYTES = V7X_VMEM_BYTES * 7 // 8

FFN_ROWS = 512
PROJ_ROWS = 512
ATTN_K_ROWS = 2816
LRU_ROWS = 512
MERGE_ROWS = 512
HALO = V7X_SUBLANES


def _tile(length, target, align):
    best = None
    for t in range(align, min(length, target) + 1, align):
        if length % t == 0:
            best = t
    if best is None:
        raise ValueError(f"no {align}-aligned tile divides {length}")
    return best


def _params(*semantics):
    return pltpu.CompilerParams(dimension_semantics=semantics, vmem_limit_bytes=VMEM_LIMIT_BYTES)


def _resident(shape):
    zeros = (0,) * len(shape)
    return pl.BlockSpec(shape, lambda *_: zeros, pipeline_mode=pl.Buffered(1))


def _rms(t):
    return t * lax.rsqrt(jnp.mean(t * t, axis=-1, keepdims=True) + EPS)


def _ada_norm(x, g, shift, scale):
    return (_rms(x) * g) * (1.0 + scale) + shift


def _gelu_tanh(x):
    cdf = 0.5 * (1.0 + jnp.tanh(math.sqrt(2.0 / math.pi) * (x + 0.044715 * (x * x * x))))
    return x * cdf


def _mod_kernel(c_ref, w_ref, b_ref, o_ref):
    c = c_ref[...]
    a = (c * jax.nn.sigmoid(c)).astype(BF16)
    o_ref[0] = jnp.dot(a, w_ref[0].astype(BF16), preferred_element_type=F32) + b_ref[0]


def _modulation(c_all, w_mod, b_mod):
    depth, d, n = w_mod.shape
    rows = c_all.shape[0]
    tn = _tile(n, d, V7X_LANES)
    out = pl.pallas_call(
        _mod_kernel,
        grid=(depth, n // tn),
        in_specs=[
            pl.BlockSpec((rows, d), lambda l, j: (0, 0)),
            pl.BlockSpec((1, d, tn), lambda l, j: (l, 0, j)),
            pl.BlockSpec((1, 1, tn), lambda l, j: (l, 0, j)),
        ],
        out_specs=pl.BlockSpec((1, rows, tn), lambda l, j: (l, 0, j)),
        out_shape=jax.ShapeDtypeStruct((depth, rows, n), F32),
        compiler_params=_params("parallel", "parallel"),
        name="modulation",
    )(c_all, w_mod, b_mod.reshape(depth, 1, n))
    return out.reshape(depth, rows, N_MOD, d)


def _ffn_block(x, mod_ref, row0, g_ref, win_ref, wout_ref):
    d_ff = wout_ref.shape[0]
    shift = mod_ref[0, row0:row0 + 1, :]
    scale = mod_ref[0, row0 + 1:row0 + 2, :]
    gate = mod_ref[0, row0 + 2:row0 + 3, :]
    h = _ada_norm(x, g_ref[...], shift, scale).astype(BF16)
    y = jnp.dot(h, win_ref[...], preferred_element_type=F32)
    g, up = y[:, :d_ff], y[:, d_ff:]
    a = (g * jax.nn.sigmoid(g) * up).astype(BF16)
    out = jnp.dot(a, wout_ref[...], preferred_element_type=F32)
    return x + (0.5 * gate) * out


def _ffn_kernel(x_ref, mod_ref, g_ref, win_ref, wout_ref, *rest, row0, final):
    o_ref = rest[-1]
    x_new = _ffn_block(x_ref[0], mod_ref, row0, g_ref, win_ref, wout_ref)
    if final:
        x_new = _rms(x_new) * rest[0][...]
    o_ref[0] = x_new


def _ffn(x, mods_l, mod_row, row0, g, w_in, w_out, final_g=None):
    b, n, d = x.shape
    d_ff = w_out.shape[0]
    tm = _tile(n, FFN_ROWS, V7X_SUBLANES)
    final = final_g is not None
    in_specs = [
        pl.BlockSpec((1, tm, d), lambda bi, i: (bi, i, 0)),
        pl.BlockSpec((1, N_MOD, d), lambda bi, i: (mod_row(bi), 0, 0)),
        _resident((1, d)),
        _resident((d, 2 * d_ff)),
        _resident((d_ff, d)),
    ]
    args = [x, mods_l, g.reshape(1, d), w_in, w_out]
    if final:
        in_specs.append(_resident((1, d)))
        args.append(final_g.reshape(1, d))
    return pl.pallas_call(
        functools.partial(_ffn_kernel, row0=row0, final=final),
        grid=(b, n // tm),
        in_specs=in_specs,
        out_specs=pl.BlockSpec((1, tm, d), lambda bi, i: (bi, i, 0)),
        out_shape=jax.ShapeDtypeStruct((b, n, d), F32),
        compiler_params=_params("parallel", "parallel"),
        name="ffn",
    )(*args)


def _head_norm_rope(z, g, cs, sn):
    p = ROPE_PAIRS
    ms = jnp.sum(z * z, axis=0, keepdims=True) * (1.0 / HEAD_DIM)
    zn = (z * lax.rsqrt(ms + EPS)) * g
    a1, a2, b1, b2 = zn[0:p], zn[p:2 * p], zn[2 * p:3 * p], zn[3 * p:4 * p]
    cr, cc = cs[0:p], cs[p:2 * p]
    sr, sc = sn[0:p], sn[p:2 * p]
    return jnp.concatenate(
        [a1 * cr - a2 * sr, a2 * cr + a1 * sr, b1 * cc - b2 * sc, b2 * cc + b1 * sc], axis=0)


def _inproj_kernel(x_ref, mod_ref, g_ref, wt_ref, wn_ref, qg_ref, kg_ref, cos_ref, sin_ref,
                   qt_ref, k_ref, vt_ref, yn_ref, *, n_heads, n_kv):
    x = x_ref[0]
    shift = mod_ref[0, 3:4, :]
    scale = mod_ref[0, 4:5, :]
    h = _ada_norm(x, g_ref[...], shift, scale).astype(BF16)
    yn_ref[0] = jnp.dot(h, wn_ref[...], preferred_element_type=F32)
    yt = lax.dot_general(wt_ref[...], h, (((1,), (1,)), ((), ())), preferred_element_type=F32)
    cs, sn = cos_ref[...], sin_ref[...]
    tm = x.shape[0]
    kv_rows = n_kv * HEAD_DIM
    for hd in range(n_heads):
        q = _head_norm_rope(yt[hd * HEAD_DIM:(hd + 1) * HEAD_DIM], qg_ref[...], cs, sn)
        q = (q * Q_SCALE).astype(BF16)
        grp, member = divmod(hd, GQA_GROUP)
        cols = slice(member * tm, (member + 1) * tm)
        for other in range(n_kv):
            rows = slice(other * HEAD_DIM, (other + 1) * HEAD_DIM)
            qt_ref[0, grp, rows, cols] = q if other == grp else jnp.zeros_like(q)
    k0 = n_heads * HEAD_DIM
    kt = jnp.concatenate(
        [_head_norm_rope(yt[k0 + j * HEAD_DIM:k0 + (j + 1) * HEAD_DIM], kg_ref[...], cs, sn)
         for j in range(n_kv)], axis=0)
    k_ref[0] = kt.T.astype(BF16)
    v0 = k0 + kv_rows
    ones_row = (lax.broadcasted_iota(jnp.int32, (V7X_BF16_ROWS, tm), 0) == 0).astype(BF16)
    for j in range(n_kv):
        vt_ref[0, j, 0:HEAD_DIM, :] = yt[v0 + j * HEAD_DIM:v0 + (j + 1) * HEAD_DIM].astype(BF16)
        vt_ref[0, j, HEAD_DIM:HEAD_DIM + V7X_BF16_ROWS, :] = ones_row


def _inproj(x, mods_l, mod_row, g, w_t, w_n, q_g, k_g, cos_t, sin_t, n_heads, n_kv):
    b, n, d = x.shape
    n_t, n_n = w_t.shape[0], w_n.shape[1]
    kv_rows = n_kv * HEAD_DIM
    v_rows = HEAD_DIM + V7X_BF16_ROWS
    tm = _tile(n, PROJ_ROWS, V7X_LANES)
    assert n_heads == n_kv * GQA_GROUP
    qg_t = jnp.broadcast_to(q_g[:, None], (HEAD_DIM, tm))
    kg_t = jnp.broadcast_to(k_g[:, None], (HEAD_DIM, tm))
    outs = pl.pallas_call(
        functools.partial(_inproj_kernel, n_heads=n_heads, n_kv=n_kv),
        grid=(b, n // tm),
        in_specs=[
            pl.BlockSpec((1, tm, d), lambda bi, i: (bi, i, 0)),
            pl.BlockSpec((1, N_MOD, d), lambda bi, i: (mod_row(bi), 0, 0)),
            _resident((1, d)),
            _resident((n_t, d)),
            _resident((d, n_n)),
            _resident((HEAD_DIM, tm)),
            _resident((HEAD_DIM, tm)),
            pl.BlockSpec((2 * ROPE_PAIRS, tm), lambda bi, i: (0, i)),
            pl.BlockSpec((2 * ROPE_PAIRS, tm), lambda bi, i: (0, i)),
        ],
        out_specs=[
            pl.BlockSpec((1, n_kv, kv_rows, GQA_GROUP * tm), lambda bi, i: (bi, 0, 0, i)),
            pl.BlockSpec((1, tm, kv_rows), lambda bi, i: (bi, i, 0)),
            pl.BlockSpec((1, n_kv, v_rows, tm), lambda bi, i: (bi, 0, 0, i)),
            pl.BlockSpec((1, tm, n_n), lambda bi, i: (bi, i, 0)),
        ],
        out_shape=[
            jax.ShapeDtypeStruct((b, n_kv, kv_rows, GQA_GROUP * n), BF16),
            jax.ShapeDtypeStruct((b, n, kv_rows), BF16),
            jax.ShapeDtypeStruct((b, n_kv, v_rows, n), BF16),
            jax.ShapeDtypeStruct((b, n, n_n), F32),
        ],
        compiler_params=_params("parallel", "parallel"),
        name="inproj",
    )(x, mods_l, g.reshape(1, d), w_t, w_n, qg_t, kg_t, cos_t, sin_t)
    return (*outs, tm)


def _attn_kernel(qt_ref, k_ref, vt_ref, o_ref, m_ref, acc_ref):
    j = pl.program_id(3)
    tq = o_ref.shape[1]
    qt = qt_ref[0, 0]
    sample = min(M_SAMPLE_KEYS, k_ref.shape[1])

    @pl.when(j == 0)
    def _():
        s0 = jnp.dot(k_ref[0, 0:sample, :], qt, preferred_element_type=F32)
        m_ref[...] = jnp.max(s0, axis=0, keepdims=True)
        acc_ref[...] = jnp.zeros(acc_ref.shape, F32)

    s = jnp.dot(k_ref[0], qt, preferred_element_type=F32)
    p = jnp.exp2(s - m_ref[...]).astype(BF16)
    pv = jnp.dot(vt_ref[0, 0], p, preferred_element_type=F32)
    safe = jnp.max(pv[HEAD_DIM:HEAD_DIM + 1]) <= P_SUM_LIMIT

    @pl.when(safe)
    def _():
        acc_ref[...] += pv

    @pl.when(jnp.logical_not(safe))
    def _():
        s2 = jnp.dot(k_ref[0], qt, preferred_element_type=F32)
        m_prev = m_ref[...]
        m_new = jnp.maximum(m_prev, jnp.max(s2, axis=0, keepdims=True))
        p2 = jnp.exp2(s2 - m_new).astype(BF16)
        acc_ref[...] = (acc_ref[...] * jnp.exp2(m_prev - m_new)
                        + jnp.dot(vt_ref[0, 0], p2, preferred_element_type=F32))
        m_ref[...] = m_new

    @pl.when(j == pl.num_programs(3) - 1)
    def _():
        acc = acc_ref[...]
        o = acc[0:HEAD_DIM] / acc[HEAD_DIM:HEAD_DIM + 1]
        o = jnp.concatenate([o[:, h * tq:(h + 1) * tq] for h in range(GQA_GROUP)], axis=0)
        o_ref[0] = o.T


def _attention(qt, k, vt, tq):
    b, n_kv, kv_rows, lq_g = qt.shape
    lq = lq_g // GQA_GROUP
    lk = k.shape[1]
    v_rows = vt.shape[2]
    tk = _tile(lk, ATTN_K_ROWS, V7X_LANES)
    grp_w = GQA_GROUP * HEAD_DIM
    return pl.pallas_call(
        _attn_kernel,
        grid=(b, n_kv, lq // tq, lk // tk),
        in_specs=[
            pl.BlockSpec((1, 1, kv_rows, GQA_GROUP * tq), lambda bi, g, i, j: (bi, g, 0, i)),
            pl.BlockSpec((1, tk, kv_rows), lambda bi, g, i, j: (bi, j, 0)),
            pl.BlockSpec((1, 1, v_rows, tk), lambda bi, g, i, j: (bi, g, 0, j)),
        ],
        out_specs=pl.BlockSpec((1, tq, grp_w), lambda bi, g, i, j: (bi, i, g)),
        out_shape=jax.ShapeDtypeStruct((b, lq, n_kv * grp_w), F32),
        scratch_shapes=[
            pltpu.VMEM((1, GQA_GROUP * tq), F32),
            pltpu.VMEM((v_rows, GQA_GROUP * tq), F32),
        ],
        compiler_params=_params("parallel", "parallel", "parallel", "arbitrary"),
        name="attention",
    )(qt, k, vt)


def _scan_groups(a, b, reverse):
    n, w = a.shape
    grouped = (n // V7X_SUBLANES, V7X_SUBLANES, w)
    a = a.reshape(grouped)
    b = b.reshape(grouped)
    sub = lax.broadcasted_iota(jnp.int32, grouped, 1)
    d = 1
    while d < V7X_SUBLANES:
        shift = V7X_SUBLANES - d if reverse else d
        valid = sub < V7X_SUBLANES - d if reverse else sub >= d
        a_s = pltpu.roll(a, shift, 1)
        b_s = pltpu.roll(b, shift, 1)
        b = jnp.where(valid, a * b_s + b, b)
        a = jnp.where(valid, a * a_s, a)
        d *= 2
    return a.reshape(n, w), b.reshape(n, w)


def _lru_kernel(u_ref, prev_ref, next_ref, cw_ref, cb_ref, wa_ref, ba_ref, wx_ref, bx_ref, lam_ref,
                h0_ref, h_ref, hl_ref, xe_ref, carry_ref, *, reverse):
    j = pl.program_id(1)
    nc = pl.num_programs(1)
    pos = nc - 1 - j if reverse else j
    tc = u_ref.shape[1]

    @pl.when(j == 0)
    def _():
        carry_ref[...] = h0_ref[0]

    xe_ref[0:HALO, :] = jnp.where(pos > 0, prev_ref[0], 0.0)
    xe_ref[HALO:HALO + tc, :] = u_ref[0]
    xe_ref[HALO + tc:, :] = jnp.where(pos < nc - 1, next_ref[0], 0.0)
    uc = cb_ref[...]
    for t in range(LRU_CONV_W):
        uc = uc + cw_ref[t:t + 1, :] * xe_ref[pl.ds(HALO - LRU_PAD_LEFT + t, tc), :]

    ub = uc.astype(BF16)
    r = jax.nn.sigmoid(jnp.dot(ub, wa_ref[...], preferred_element_type=F32) + ba_ref[...])
    i = jax.nn.sigmoid(jnp.dot(ub, wx_ref[...], preferred_element_type=F32) + bx_ref[...])
    nlam = -lam_ref[...]
    softplus = jnp.maximum(nlam, 0.0) + jnp.log1p(jnp.exp(-jnp.abs(nlam)))
    log_a = (-LRU_C * r) * softplus
    a = jnp.exp(log_a)
    xin = jnp.sqrt(-jnp.tanh(log_a) * (a * a + 1.0)) * i * uc

    grp_a, grp_b = _scan_groups(a, xin, reverse)
    h_in = carry_ref[...]
    n_groups = tc // V7X_SUBLANES
    for step in range(n_groups):
        g = n_groups - 1 - step if reverse else step
        rows = slice(g * V7X_SUBLANES, (g + 1) * V7X_SUBLANES)
        h = grp_a[rows] * h_in + grp_b[rows]
        h_ref[0, rows, :] = h
        h_in = h[0:1] if reverse else h[V7X_SUBLANES - 1:V7X_SUBLANES]
    carry_ref[...] = h_in
    hl_ref[0] = h_in


def _lru(yn, col, conv_w, conv_b, wa, ba, wx, bx, lam, h0, reverse):
    b, n, _ = yn.shape
    w = wa.shape[0]
    tc = _tile(n, LRU_ROWS, V7X_SUBLANES)
    nc = n // tc
    per = tc // HALO
    last_halo = n // HALO - 1

    def pos(j):
        return nc - 1 - j if reverse else j

    vec = lambda v: v.reshape(1, w)
    return pl.pallas_call(
        functools.partial(_lru_kernel, reverse=reverse),
        grid=(b, nc),
        in_specs=[
            pl.BlockSpec((1, tc, w), lambda bi, j: (bi, pos(j), col)),
            pl.BlockSpec((1, HALO, w), lambda bi, j: (bi, jnp.maximum(pos(j) * per - 1, 0), col)),
            pl.BlockSpec((1, HALO, w), lambda bi, j: (bi, jnp.minimum((pos(j) + 1) * per, last_halo), col)),
            _resident((LRU_CONV_W, w)),
            _resident((1, w)),
            _resident((w, w)),
            _resident((1, w)),
            _resident((w, w)),
            _resident((1, w)),
            _resident((1, w)),
            pl.BlockSpec((1, 1, w), lambda bi, j: (bi, 0, 0)),
        ],
        out_specs=[
            pl.BlockSpec((1, tc, w), lambda bi, j: (bi, pos(j), 0)),
            pl.BlockSpec((1, 1, w), lambda bi, j: (bi, 0, 0)),
        ],
        out_shape=[
            jax.ShapeDtypeStruct((b, n, w), F32),
            jax.ShapeDtypeStruct((b, 1, w), F32),
        ],
        scratch_shapes=[
            pltpu.VMEM((tc + 2 * HALO, w), F32),
            pltpu.VMEM((1, w), F32),
        ],
        compiler_params=_params("parallel", "arbitrary"),
        name="lru_bwd" if reverse else "lru_fwd",
    )(yn, yn, yn, conv_w, vec(conv_b), wa, vec(ba), wx, vec(bx), vec(lam), h0)


def _merge_kernel(attn_ref, hf_ref, hb_ref, gx_ref, bg_ref, cg_ref, sx_ref, cgp_ref, sxp_ref,
                  cgn_ref, sxn_ref, x_ref, mod_ref, g_ref, w_ref, cw_ref, cb_ref,
                  fg_ref, win_ref, wout_ref, *rest, final):
    o_ref, pe_ref = rest[-2:]
    i = pl.program_id(1)
    ni = pl.num_programs(1)
    tm = x_ref.shape[1]

    pe_ref[0:HALO, :] = jnp.where(i > 0, cgp_ref[0] * sxp_ref[0], 0.0)
    pe_ref[HALO:HALO + tm, :] = cg_ref[0] * sx_ref[0]
    pe_ref[HALO + tm:, :] = jnp.where(i < ni - 1, cgn_ref[0] * sxn_ref[0], 0.0)
    conv = cb_ref[...]
    for t in range(SC_CONV_W):
        conv = conv + cw_ref[t:t + 1, :] * pe_ref[pl.ds(HALO - SC_PAD_LEFT + t, tm), :]
    sc = bg_ref[0] * conv

    lru = (hf_ref[0] + hb_ref[0]) * _gelu_tanh(gx_ref[0])
    cat = jnp.concatenate([_rms(attn_ref[0]), _rms(lru), _rms(sc)], axis=-1) * g_ref[...]
    out = jnp.dot(cat.astype(BF16), w_ref[...], preferred_element_type=F32)
    x_mix = x_ref[0] + mod_ref[0, 5:6, :] * out
    x_new = _ffn_block(x_mix, mod_ref, 6, fg_ref, win_ref, wout_ref)
    if final:
        x_new = _rms(x_new) * rest[0][...]
    o_ref[0] = x_new


def _merge_ffn(attn, hf, hb, yn, x, mods_l, mod_row, grp_g, w_out, sc_w, sc_b, ffn_g, w_ffn_in, w_ffn_out,
               final_g=None):
    b, n, d = x.shape
    w = hf.shape[2]
    aw = attn.shape[2]
    d_ff = w_ffn_out.shape[0]
    final = final_g is not None
    tm = _tile(n, MERGE_ROWS, V7X_SUBLANES)
    per = tm // HALO
    last_halo = n // HALO - 1
    main = lambda col: pl.BlockSpec((1, tm, w), lambda bi, i: (bi, i, col))
    prev = lambda col: pl.BlockSpec((1, HALO, w), lambda bi, i: (bi, jnp.maximum(i * per - 1, 0), col))
    nxt = lambda col: pl.BlockSpec((1, HALO, w), lambda bi, i: (bi, jnp.minimum((i + 1) * per, last_halo), col))
    in_specs = [
        pl.BlockSpec((1, tm, aw), lambda bi, i: (bi, i, 0)),
        main(0), main(0),
        main(1), main(2), main(3), main(4),
        prev(3), prev(4), nxt(3), nxt(4),
        pl.BlockSpec((1, tm, d), lambda bi, i: (bi, i, 0)),
        pl.BlockSpec((1, N_MOD, d), lambda bi, i: (mod_row(bi), 0, 0)),
        _resident((1, d)),
        _resident((d, d)),
        _resident((SC_CONV_W, w)),
        _resident((1, w)),
        _resident((1, d)),
        _resident((d, 2 * d_ff)),
        _resident((d_ff, d)),
    ]
    args = [attn, hf, hb, yn, yn, yn, yn, yn, yn, yn, yn, x, mods_l, grp_g.reshape(1, d), w_out, sc_w,
            sc_b.reshape(1, w), ffn_g.reshape(1, d), w_ffn_in, w_ffn_out]
    if final:
        in_specs.append(_resident((1, d)))
        args.append(final_g.reshape(1, d))
    return pl.pallas_call(
        functools.partial(_merge_kernel, final=final),
        grid=(b, n // tm),
        in_specs=in_specs,
        out_specs=pl.BlockSpec((1, tm, d), lambda bi, i: (bi, i, 0)),
        out_shape=jax.ShapeDtypeStruct((b, n, d), F32),
        scratch_shapes=[pltpu.VMEM((tm + 2 * HALO, w), F32)],
        compiler_params=_params("parallel", "parallel"),
        name="merge_ffn",
    )(*args)


def _rope_tables_t(seq):
    rows = seq // GRID_W
    row_ids = jnp.repeat(jnp.arange(rows), GRID_W).astype(F32)
    col_ids = jnp.tile(jnp.arange(GRID_W), rows).astype(F32)
    inv_freq = ROPE_THETA ** (-jnp.arange(ROPE_PAIRS, dtype=F32) / ROPE_PAIRS)
    ang = jnp.stack([row_ids[:, None] * inv_freq, col_ids[:, None] * inv_freq], axis=1)
    ang_t = ang.reshape(seq, 2 * ROPE_PAIRS).T
    return jnp.cos(ang_t), jnp.sin(ang_t)


def _block_diag(w):
    nb, n, _ = w.shape
    eye = jnp.eye(nb, dtype=w.dtype)
    return (eye[:, None, :, None] * w[:, :, None, :]).reshape(nb * n, nb * n)


def kernel(x, c, ctx, c_ctx, w_mod, b_mod, norm_g, w_ffn_in, w_ffn_out, w_in, q_norm_g, k_norm_g,
           lru_conv_w, lru_conv_b, lru_wa, lru_ba, lru_wx, lru_bx, lru_lambda, sc_conv_w, sc_conv_b,
           grp_norm_g, w_out, final_norm_g):
    bsz, seq, d = x.shape
    n_ctx = ctx.shape[1]
    depth = w_mod.shape[0]
    attn_w = d // 2
    n_heads = attn_w // HEAD_DIM
    n_kv = n_heads // GQA_GROUP
    kv_w = n_kv * HEAD_DIM
    lru_w = lru_conv_w.shape[2]
    n_t = attn_w + 2 * kv_w

    rows = -(-(bsz + 1) // V7X_SUBLANES) * V7X_SUBLANES
    c_all = jnp.zeros((rows, d), F32).at[:bsz].set(c).at[bsz].set(c_ctx)
    mods = _modulation(c_all, w_mod, b_mod)
    x_row = lambda bi: bi
    ctx_row = lambda bi: bsz

    cos_x, sin_x = _rope_tables_t(seq)
    cos_c = jnp.ones((2 * ROPE_PAIRS, n_ctx), F32)
    sin_c = jnp.zeros((2 * ROPE_PAIRS, n_ctx), F32)
    zero_state = jnp.zeros((bsz, 1, lru_w), F32)

    h_ctx = ctx
    for l in range(depth):
        last = l == depth - 1
        wf_in = w_ffn_in[l].astype(BF16)
        wf_out = w_ffn_out[l].astype(BF16)
        w_t = w_in[l][:, :n_t].T.astype(BF16)
        w_n = w_in[l][:, n_t:].astype(BF16)
        wo = w_out[l].astype(BF16)
        lru_p = [(lru_conv_w[l], lru_conv_b[l], _block_diag(lru_wa[l, dr]).astype(BF16), lru_ba[l, dr],
                  _block_diag(lru_wx[l, dr]).astype(BF16), lru_bx[l, dr], lru_lambda[l, dr]) for dr in range(2)]

        x = _ffn(x, mods[l], x_row, 0, norm_g[l, 0], wf_in[0], wf_out[0])
        h_ctx = _ffn(h_ctx, mods[l], ctx_row, 0, norm_g[l, 0], wf_in[0], wf_out[0])

        proj = functools.partial(_inproj, g=norm_g[l, 1], w_t=w_t, w_n=w_n, q_g=q_norm_g[l], k_g=k_norm_g[l],
                                 n_heads=n_heads, n_kv=n_kv)
        qt_x, k_x, vt_x, yn_x, tq_x = proj(x, mods[l], x_row, cos_t=cos_x, sin_t=sin_x)
        qt_c, k_c, vt_c, yn_c, tq_c = proj(h_ctx, mods[l], ctx_row, cos_t=cos_c, sin_t=sin_c)

        attn_x = _attention(qt_x, jnp.concatenate([k_c, k_x], axis=1), jnp.concatenate([vt_c, vt_x], axis=3),
                            tq_x)

        hc_f, state_f = _lru(yn_c, 0, *lru_p[0], zero_state, reverse=False)
        hc_b, state_b = _lru(yn_c, 0, *lru_p[1], zero_state, reverse=True)
        hx_f, _ = _lru(yn_x, 0, *lru_p[0], state_f, reverse=False)
        hx_b, _ = _lru(yn_x, 0, *lru_p[1], state_b, reverse=True)

        x = _merge_ffn(attn_x, hx_f, hx_b, yn_x, x, mods[l], x_row, grp_norm_g[l], wo, sc_conv_w[l],
                       sc_conv_b[l], norm_g[l, 2], wf_in[1], wf_out[1], final_g=final_norm_g if last else None)
        if not last:
            attn_c = _attention(qt_c, k_c, vt_c, tq_c)
            h_ctx = _merge_ffn(attn_c, hc_f, hc_b, yn_c, h_ctx, mods[l], ctx_row, grp_norm_g[l], wo,
                               sc_conv_w[l], sc_conv_b[l], norm_g[l, 2], wf_in[1], wf_out[1])
    return x
```
